```python
import jax, jax.numpy as jnp
from jax import lax
import numpy as np

D_MODEL = 2048
BATCH = 4
SEQ = 4096
DEPTH = 2

N_MIXERS = 4
GROUP_WIDTH = D_MODEL // N_MIXERS
GDN_HEADS = 4
GDN_HEAD_DIM = GROUP_WIDTH // GDN_HEADS
MLSTM_HEADS = 4
MLSTM_HEAD_DIM = GROUP_WIDTH // MLSTM_HEADS
RGLRU_BLOCKS = 4
RGLRU_BLOCK_DIM = GROUP_WIDTH // RGLRU_BLOCKS
RGLRU_C = 8.0
RWKV_HEAD_DIM = 64
RWKV_HEADS = GROUP_WIDTH // RWKV_HEAD_DIM
RWKV_DECAY_LORA = 64
RWKV_A_LORA = 64
RWKV_GATE_LORA = 128
CONV_WIDTH = 4
CHUNK = 64
FFN_HIDDEN = -(-8 * D_MODEL // (3 * 256)) * 256
NORM_EPS = 1e-6
RWKV_LN_EPS = 64e-5

GDN_COLS = 4 * GROUP_WIDTH + 2 * GDN_HEADS
MLSTM_COLS = 4 * GROUP_WIDTH + 2 * MLSTM_HEADS
RGLRU_COLS = 2 * GROUP_WIDTH
RWKV_COLS = 3 * GROUP_WIDTH + RWKV_DECAY_LORA + RWKV_A_LORA + RWKV_GATE_LORA
IN_COLS = GDN_COLS + MLSTM_COLS + RGLRU_COLS + RWKV_COLS
MIXER_SPLITS = [GDN_COLS, GDN_COLS + MLSTM_COLS, GDN_COLS + MLSTM_COLS + RGLRU_COLS]

kernel_name = 'hybrid_parallel_heads_gdn_mlstm_rglru_rwkv7'


def rmsnorm(x, w, eps=NORM_EPS):
    xf = x.astype(jnp.float32)
    y = xf * lax.rsqrt(jnp.mean(xf * xf, axis=-1, keepdims=True) + eps)
    return (y * w.astype(jnp.float32)).astype(x.dtype)


def l2norm(x, eps=1e-6):
    xf = x.astype(jnp.float32)
    return xf * lax.rsqrt(jnp.sum(xf * xf, axis=-1, keepdims=True) + eps)


def head_layernorm(y, w, b, eps):
    mu = jnp.mean(y, axis=-1, keepdims=True)
    var = jnp.mean(jnp.square(y - mu), axis=-1, keepdims=True)
    return (y - mu) * lax.rsqrt(var + eps) * w + b


def causal_conv(x, w):
    return lax.conv_general_dilated(
        x, w[:, None, :].astype(x.dtype), window_strides=(1,), padding=[(w.shape[0] - 1, 0)],
        dimension_numbers=('NWC', 'WIO', 'NWC'), feature_group_count=x.shape[-1])


def heads(t, h):
    return t.reshape(t.shape[:-1] + (h, t.shape[-1] // h))


def to_chunks(t):
    b, T, h = t.shape[:3]
    t = t.reshape((b, T // CHUNK, CHUNK, h) + t.shape[3:])
    return jnp.transpose(t, (1, 0, 3, 2) + tuple(range(4, t.ndim)))


def from_chunks(t):
    nc, b, h, c, d = t.shape
    return jnp.transpose(t, (1, 0, 3, 2, 4)).reshape(b, nc * c, h, d)


def gated_delta_rule(q, k, v, g, beta):
    q, k, v = [to_chunks(t.astype(jnp.float32)) for t in (q, k, v)]
    g, beta = [to_chunks(t.astype(jnp.float32)) for t in (g, beta)]
    g = jnp.cumsum(g, axis=-1)
    causal = jnp.tril(jnp.ones((CHUNK, CHUNK), dtype=bool))
    decay = jnp.exp(jnp.where(causal, g[..., :, None] - g[..., None, :], -jnp.inf))
    kb = k * beta[..., None]
    lower = jnp.tril(jnp.einsum('nbhcd,nbhsd->nbhcs', kb, k) * decay, -1)
    eye = jnp.eye(CHUNK, dtype=jnp.float32)
    dv = v.shape[-1]
    rhs = jnp.concatenate([v * beta[..., None], kb * jnp.exp(g)[..., None]], axis=-1)
    sol = lax.linalg.triangular_solve(lower + eye, rhs, left_side=True, lower=True, unit_diagonal=True)
    u, w = sol[..., :dv], sol[..., dv:]
    attn = jnp.einsum('nbhcd,nbhsd->nbhcs', q, k) * decay
    q_dec = q * jnp.exp(g)[..., None]
    k_dec = k * jnp.exp(g[..., -1:] - g)[..., None]
    chunk_decay = jnp.exp(g[..., -1])

    def step(S, xs):
        u_c, w_c, attn_c, q_c, k_c, cd = xs
        v_new = u_c - jnp.einsum('bhcd,bhde->bhce', w_c, S)
        o = jnp.einsum('bhcd,bhde->bhce', q_c, S) + jnp.einsum('bhcs,bhse->bhce', attn_c, v_new)
        S = S * cd[..., None, None] + jnp.einsum('bhcd,bhce->bhde', k_c, v_new)
        return S, o

    S0 = jnp.zeros(q.shape[1:3] + (q.shape[-1], dv), jnp.float32)
    _, o = lax.scan(step, S0, (u, w, attn, q_dec, k_dec, chunk_decay))
    return from_chunks(o)


def gated_deltanet(u, conv_w, a_log, dt_bias, norm_w):
    W, H = GROUP_WIDTH, GDN_HEADS
    qkv, z, a_raw, b_raw = jnp.split(u, [3 * W, 4 * W, 4 * W + H], axis=-1)
    qkv = jax.nn.silu(causal_conv(qkv, conv_w))
    q, k, v = [heads(t, H) for t in jnp.split(qkv, 3, axis=-1)]
    q = l2norm(q) * GDN_HEAD_DIM ** -0.5
    k = l2norm(k)
    beta = jax.nn.sigmoid(b_raw.astype(jnp.float32))
    g = -jnp.exp(a_log.astype(jnp.float32)) * jax.nn.softplus(a_raw.astype(jnp.float32) + dt_bias)
    o = gated_delta_rule(q, k, v, g, beta)
    o = rmsnorm(o, norm_w) * jax.nn.silu(heads(z, H).astype(jnp.float32))
    return o.reshape(o.shape[:2] + (W,))


def mlstm_chunkwise(q, k, v, i_pre, log_f):
    q, k, v = [to_chunks(t.astype(jnp.float32)) for t in (q, k, v)]
    i_pre, log_f = [to_chunks(t) for t in (i_pre, log_f)]
    causal = jnp.tril(jnp.ones((CHUNK, CHUNK), dtype=bool))

    def step(carry, xs):
        C, n, m = carry
        q_c, k_c, v_c, i_c, f_c = xs
        b = jnp.cumsum(f_c, axis=-1)
        d = jnp.where(causal, b[..., :, None] - b[..., None, :] + i_c[..., None, :], -jnp.inf)
        inter = b + m[..., None]
        m_t = jnp.maximum(inter, jnp.max(d, axis=-1))
        s = jnp.einsum('bhtd,bhsd->bhts', q_c, k_c) * jnp.exp(d - m_t[..., None])
        a = jnp.exp(inter - m_t)
        num = a[..., None] * jnp.einsum('bhtd,bhde->bhte', q_c, C) + jnp.einsum('bhts,bhse->bhte', s, v_c)
        den = a * jnp.einsum('bhtd,bhd->bht', q_c, n) + jnp.sum(s, axis=-1)
        h = num / jnp.maximum(jnp.abs(den), jnp.exp(-m_t))[..., None]
        g = b[..., -1]
        w_log = g[..., None] - b + i_c
        m_new = jnp.maximum(g + m, jnp.max(w_log, axis=-1))
        scale_old = jnp.exp(g + m - m_new)
        w_in = jnp.exp(w_log - m_new[..., None])
        C = scale_old[..., None, None] * C + jnp.einsum('bhsd,bhse->bhde', k_c * w_in[..., None], v_c)
        n = scale_old[..., None] * n + jnp.einsum('bhsd,bhs->bhd', k_c, w_in)
        return (C, n, m_new), h

    bsz, hh, dk, dv = q.shape[1], q.shape[2], q.shape[-1], v.shape[-1]
    carry0 = (jnp.zeros((bsz, hh, dk, dv), jnp.float32), jnp.zeros((bsz, hh, dk), jnp.float32),
              jnp.zeros((bsz, hh), jnp.float32))
    _, h = lax.scan(step, carry0, (q, k, v, i_pre, log_f))
    return from_chunks(h)


def mlstm(u, conv_w, b_i, b_f, norm_w):
    W, H = GROUP_WIDTH, MLSTM_HEADS
    qk, v, o_raw, i_raw, f_raw = jnp.split(u, [2 * W, 3 * W, 4 * W, 4 * W + H], axis=-1)
    qk = jax.nn.silu(causal_conv(qk, conv_w))
    q, k = [heads(t, H) for t in jnp.split(qk, 2, axis=-1)]
    k = k * MLSTM_HEAD_DIM ** -0.5
    i_pre = i_raw.astype(jnp.float32) + b_i
    log_f = jax.nn.log_sigmoid(f_raw.astype(jnp.float32) + b_f)
    h_tilde = mlstm_chunkwise(q, k, heads(v, H), i_pre, log_f)
    h = jax.nn.sigmoid(heads(o_raw, H).astype(jnp.float32)) * h_tilde
    h = rmsnorm(h, norm_w)
    return h.reshape(h.shape[:2] + (W,))


def linear_scan_combine(left, right):
    a_l, b_l = left
    a_r, b_r = right
    return a_l * a_r, a_r * b_l + b_r


def rglru_block(u, conv_w, conv_b, w_a, b_a, w_x, b_x, lam):
    xb, gate = jnp.split(u, 2, axis=-1)
    xb = causal_conv(xb, conv_w) + conv_b
    xh = heads(xb, RGLRU_BLOCKS)
    r = jax.nn.sigmoid((jnp.einsum('btni,nij->btnj', xh, w_a).reshape(xb.shape) + b_a).astype(jnp.float32))
    i = jax.nn.sigmoid((jnp.einsum('btni,nij->btnj', xh, w_x).reshape(xb.shape) + b_x).astype(jnp.float32))
    log_a = -RGLRU_C * r * jax.nn.softplus(-lam.astype(jnp.float32))
    a = jnp.exp(log_a)
    inp = jnp.sqrt(-jnp.expm1(2.0 * log_a)) * (i * xb.astype(jnp.float32))
    _, h = lax.associative_scan(linear_scan_combine, (a, inp), axis=1)
    return h * jax.nn.gelu(gate.astype(jnp.float32))


def rwkv7_scan(r, w, k, v, kk, a):
    xs = [jnp.swapaxes(t.astype(jnp.float32), 0, 1) for t in (r, w, k, v, -kk, kk * a)]

    def step(S, inp):
        r_t, w_t, k_t, v_t, a_t, b_t = inp
        sa = jnp.einsum('bhij,bhj->bhi', S, a_t)
        S = S * w_t[:, :, None, :] + sa[..., None] * b_t[:, :, None, :] + v_t[..., None] * k_t[:, :, None, :]
        return S, jnp.einsum('bhij,bhj->bhi', S, r_t)

    S0 = jnp.zeros(r.shape[:1] + r.shape[2:] + r.shape[-1:], jnp.float32)
    _, y = lax.scan(step, S0, xs)
    return jnp.swapaxes(y, 0, 1)


def rwkv7_time_mix(u, mu, w0, w2, a0, a2, g2, k_k, k_a, r_k, ln_w, ln_b):
    W, H = GROUP_WIDTH, RWKV_HEADS
    u_prev = jnp.pad(u, ((0, 0), (1, 0), (0, 0)))[:, :-1]
    u = u + (u_prev - u) * mu
    r, k, v, w_lo, a_lo, g_lo = jnp.split(
        u, [W, 2 * W, 3 * W, 3 * W + RWKV_DECAY_LORA, 3 * W + RWKV_DECAY_LORA + RWKV_A_LORA], axis=-1)
    w_raw = -jax.nn.softplus(-(w0 + jnp.tanh(w_lo) @ w2).astype(jnp.float32)) - 0.5
    decay = jnp.exp(-jnp.exp(w_raw))
    a = jax.nn.sigmoid((a0 + a_lo @ a2).astype(jnp.float32))
    g = (jax.nn.sigmoid(g_lo) @ g2).astype(jnp.float32)
    kk = l2norm(heads(k * k_k, H))
    k = k.astype(jnp.float32) * (1.0 + (a - 1.0) * k_a)
    r, k, v, decay, a = [heads(t.astype(jnp.float32), H) for t in (r, k, v, decay, a)]
    y = rwkv7_scan(r, decay, k, v, kk, a)
    y = head_layernorm(y, ln_w, ln_b, RWKV_LN_EPS)
    y = y + jnp.sum(r * k * r_k, axis=-1, keepdims=True) * v
    return y.reshape(y.shape[:2] + (W,)) * g


def setup_inputs(seed: int = 0) -> dict:
    key = jax.random.key(seed)
    ks = iter(jax.random.split(key, 48))
    L, D, W = DEPTH, D_MODEL, GROUP_WIDTH
    f32 = jnp.float32

    def normal(shape, scale):
        return jax.random.normal(next(ks), shape, f32) * scale

    def uniform(shape, lo, hi):
        return jax.random.uniform(next(ks), shape, f32, lo, hi)

    def gain(shape):
        return 1.0 + normal(shape, 0.02)

    dt = jnp.exp(uniform((L, GDN_HEADS), float(np.log(1e-3)), float(np.log(0.1))))
    a_pow = uniform((L, W), 0.9, 0.999) ** (1.0 / RGLRU_C)
    return {
        'x': normal((BATCH, SEQ, D), 1.0),
        'attn_norm': gain((L, D)),
        'w_in': normal((L, D, IN_COLS), D ** -0.5),
        'gdn_conv': normal((L, CONV_WIDTH, 3 * W), CONV_WIDTH ** -0.5),
        'gdn_a_log': jnp.log(uniform((L, GDN_HEADS), 1.0, 16.0)),
        'gdn_dt_bias': dt + jnp.log(-jnp.expm1(-dt)),
        'gdn_norm': gain((L, GDN_HEAD_DIM)),
        'mlstm_conv': normal((L, CONV_WIDTH, 2 * W), CONV_WIDTH ** -0.5),
        'mlstm_b_i': normal((L, MLSTM_HEADS), 0.1),
        'mlstm_b_f': uniform((L, MLSTM_HEADS), 3.0, 6.0),
        'mlstm_norm': gain((L, MLSTM_HEADS, MLSTM_HEAD_DIM)),
        'rglru_conv': normal((L, CONV_WIDTH, W), CONV_WIDTH ** -0.5),
        'rglru_conv_b': normal((L, W), 0.01),
        'rglru_w_a': normal((L, RGLRU_BLOCKS, RGLRU_BLOCK_DIM, RGLRU_BLOCK_DIM), RGLRU_BLOCK_DIM ** -0.5),
        'rglru_b_a': normal((L, W), 0.01),
        'rglru_w_x': normal((L, RGLRU_BLOCKS, RGLRU_BLOCK_DIM, RGLRU_BLOCK_DIM), RGLRU_BLOCK_DIM ** -0.5),
        'rglru_b_x': normal((L, W), 0.01),
        'rglru_lambda': jnp.log(a_pow) - jnp.log1p(-a_pow),
        'rwkv_mu': uniform((L, RWKV_COLS), 0.0, 1.0),
        'rwkv_w0': uniform((L, W), -6.0, 1.0),
        'rwkv_w2': normal((L, RWKV_DECAY_LORA, W), 0.1),
        'rwkv_a0': normal((L, W), 0.1),
        'rwkv_a2': normal((L, RWKV_A_LORA, W), 0.5 * RWKV_A_LORA ** -0.5),
        'rwkv_g2': normal((L, RWKV_GATE_LORA, W), RWKV_GATE_LORA ** -0.5),
        'rwkv_k_k': 0.85 + normal((L, W), 0.05),
        'rwkv_k_a': 1.0 + normal((L, W), 0.05),
        'rwkv_r_k': normal((L, RWKV_HEADS, RWKV_HEAD_DIM), 0.1),
        'rwkv_ln_w': gain((L, RWKV_HEADS, RWKV_HEAD_DIM)),
        'rwkv_ln_b': normal((L, RWKV_HEADS, RWKV_HEAD_DIM), 0.01),
        'w_out': normal((L, N_MIXERS * W, D), (N_MIXERS * W) ** -0.5),
        'ffn_norm': gain((L, D)),
        'ffn_w_gate': normal((L, D, FFN_HIDDEN), D ** -0.5),
        'ffn_w_up': normal((L, D, FFN_HIDDEN), D ** -0.5),
        'ffn_w_down': normal((L, FFN_HIDDEN, D), FFN_HIDDEN ** -0.5),
        'final_norm': gain((D,)),
    }


def reference(x, attn_norm, w_in, gdn_conv, gdn_a_log, gdn_dt_bias, gdn_norm,
              mlstm_conv, mlstm_b_i, mlstm_b_f, mlstm_norm,
              rglru_conv, rglru_conv_b, rglru_w_a, rglru_b_a, rglru_w_x, rglru_b_x, rglru_lambda,
              rwkv_mu, rwkv_w0, rwkv_w2, rwkv_a0, rwkv_a2, rwkv_g2, rwkv_k_k, rwkv_k_a, rwkv_r_k,
              rwkv_ln_w, rwkv_ln_b, w_out, ffn_norm, ffn_w_gate, ffn_w_up, ffn_w_down, final_norm):
    for l in range(DEPTH):
        n = rmsnorm(x, attn_norm[l])
        u = n @ w_in[l]
        u_gdn, u_mlstm, u_rglru, u_rwkv = jnp.split(u, MIXER_SPLITS, axis=-1)
        y_gdn = gated_deltanet(u_gdn, gdn_conv[l], gdn_a_log[l], gdn_dt_bias[l], gdn_norm[l])
        y_mlstm = mlstm(u_mlstm, mlstm_conv[l], mlstm_b_i[l], mlstm_b_f[l], mlstm_norm[l])
        y_rglru = rglru_block(u_rglru, rglru_conv[l], rglru_conv_b[l], rglru_w_a[l], rglru_b_a[l],
                              rglru_w_x[l], rglru_b_x[l], rglru_lambda[l])
        y_rwkv = rwkv7_time_mix(u_rwkv, rwkv_mu[l], rwkv_w0[l], rwkv_w2[l], rwkv_a0[l], rwkv_a2[l],
                                rwkv_g2[l], rwkv_k_k[l], rwkv_k_a[l], rwkv_r_k[l], rwkv_ln_w[l], rwkv_ln_b[l])
        mix = jnp.concatenate([y.astype(x.dtype) for y in (y_gdn, y_mlstm, y_rglru, y_rwkv)], axis=-1)
        x = x + mix @ w_out[l]
        n = rmsnorm(x, ffn_norm[l])
        x = x + (jax.nn.silu(n @ ffn_w_gate[l]) * (n @ ffn_w_up[l])) @ ffn_w_down[l]
    return rmsnorm(x, final_norm)
```

```python
import functools

import jax
import jax.numpy as jnp
from jax import lax
from jax.experimental import pallas as pl
from jax.experimental.pallas import tpu as pltpu

F32 = jnp.float32
BF16 = jnp.bfloat16

D_MODEL = 2048
GROUP_W = 512
HEAD_D = 128
N_HEADS = 4
RWKV_N = 64
CONV_K = 4
CHUNK = 64
FFN_HIDDEN = 5632
NORM_EPS = 1e-6
RWKV_LN_EPS = 64e-5
RGLRU_C = 8.0
LANES = 128
HALO = 8

U_COLS = 7168
GDN_BLK = 0
MLSTM_BLK = 1
RWKV_BLK = 2
GATES_BLK = 46
RGLRU_BLK = 6
RWKV_USED = 1792

VMEM_LIMIT = 56 * 1024 * 1024

_NN = (((1,), (0,)), ((), ()))
_NT = (((1,), (1,)), ((), ()))
_TN = (((0,), (0,)), ((), ()))


def _mm(a, b, dims=_NN):
    return lax.dot_general(a.astype(BF16), b.astype(BF16), dims, preferred_element_type=F32)


def _mm_exact(a, b, dims=_NN):
    return lax.dot_general(a, b, dims, precision=lax.Precision.HIGHEST, preferred_element_type=F32)


def _sigmoid(x):
    return 1.0 / (1.0 + jnp.exp(-x))


def _softplus(x):
    return jnp.maximum(x, 0.0) + jnp.log1p(jnp.exp(-jnp.abs(x)))


def _silu(x):
    return x * _sigmoid(x)


def _rms(x, w, eps=NORM_EPS):
    return x * lax.rsqrt(jnp.mean(x * x, axis=-1, keepdims=True) + eps) * w


def _chunk_cumsum(x):
    row = lax.broadcasted_iota(jnp.int32, x.shape, 0) % CHUNK
    d = 1
    while d < CHUNK:
        x = x + jnp.where(row >= d, pltpu.roll(x, d, 0), 0.0)
        d *= 2
    return x


def _inv_unit_lower(a_strict, levels):
    n = a_strict.shape[0]
    eye = (lax.broadcasted_iota(jnp.int32, (n, n), 0) == lax.broadcasted_iota(jnp.int32, (n, n), 1)).astype(F32)
    p = -a_strict
    inv = eye + p
    for _ in range(levels):
        p = _mm_exact(p, p)
        inv = inv + _mm_exact(inv, p)
    return inv


def _causal_conv(buf, x, width, conv_w):
    tt = x.shape[0]
    buf[HALO:HALO + tt, :] = x[:, :width]
    acc = None
    for j in range(CONV_K):
        lo = HALO - (CONV_K - 1) + j
        term = conv_w[j:j + 1, :] * buf[lo:lo + tt, :]
        acc = term if acc is None else acc + term
    buf[0:HALO, :] = x[tt - HALO:tt, :width]
    return acc


def _inproj_kernel(x_ref, nw_ref, w_ref, o_ref, n_scr):
    @pl.when(pl.program_id(1) == 0)
    def _():
        n_scr[...] = _rms(x_ref[...], nw_ref[...]).astype(BF16)

    o_ref[...] = jnp.dot(n_scr[...], w_ref[...], preferred_element_type=F32)


def _inproj(x2d, norm_w, w_perm, *, tm=512, tn=1024):
    n_tok = x2d.shape[0]
    return pl.pallas_call(
        _inproj_kernel,
        out_shape=jax.ShapeDtypeStruct((n_tok, U_COLS), F32),
        grid=(n_tok // tm, U_COLS // tn),
        in_specs=[
            pl.BlockSpec((tm, D_MODEL), lambda i, j: (i, 0)),
            pl.BlockSpec((1, D_MODEL), lambda i, j: (0, 0)),
            pl.BlockSpec((D_MODEL, tn), lambda i, j: (0, j)),
        ],
        out_specs=pl.BlockSpec((tm, tn), lambda i, j: (i, j)),
        scratch_shapes=[pltpu.VMEM((tm, D_MODEL), BF16)],
        compiler_params=pltpu.CompilerParams(
            dimension_semantics=("parallel", "arbitrary"), vmem_limit_bytes=VMEM_LIMIT),
        name="inproj",
    )(x2d, norm_w.reshape(1, D_MODEL), w_perm)


def _gdn_kernel(u_ref, gt_ref, cw_ref, prm_ref, o_ref, buf, s_ref, *, n_chunks):
    @pl.when(pl.program_id(1) == 0)
    def _():
        buf[0:HALO, :] = jnp.zeros((HALO, 3 * GROUP_W), F32)
        s_ref[...] = jnp.zeros_like(s_ref)

    x = u_ref[0]
    qkv = _silu(_causal_conv(buf, x, 3 * GROUP_W, cw_ref[...]))
    gates = gt_ref[0]
    prm = prm_ref[...]
    g_all = -jnp.exp(prm[0:1, :]) * _softplus(gates + prm[1:2, :])
    beta_all = _sigmoid(gates)
    gc_all = _chunk_cumsum(g_all)
    norm_w = prm[2:3, :]

    ri = lax.broadcasted_iota(jnp.int32, (CHUNK, CHUNK), 0)
    ci = lax.broadcasted_iota(jnp.int32, (CHUNK, CHUNK), 1)
    causal = ri >= ci
    strict = ri > ci

    for c in range(n_chunks):
        rows = slice(c * CHUNK, (c + 1) * CHUNK)
        gc_c = gc_all[rows, :]
        gc_t = gc_c.T
        outs = []
        for h in range(N_HEADS):
            hs = slice(h * HEAD_D, (h + 1) * HEAD_D)
            q = qkv[rows, h * HEAD_D:(h + 1) * HEAD_D]
            k = qkv[rows, GROUP_W + h * HEAD_D:GROUP_W + (h + 1) * HEAD_D]
            v = qkv[rows, 2 * GROUP_W + h * HEAD_D:2 * GROUP_W + (h + 1) * HEAD_D]
            q = q * lax.rsqrt(jnp.sum(q * q, axis=-1, keepdims=True) + 1e-6) * (HEAD_D ** -0.5)
            k = k * lax.rsqrt(jnp.sum(k * k, axis=-1, keepdims=True) + 1e-6)
            beta = beta_all[rows, N_HEADS + h:N_HEADS + h + 1]
            gcol = gc_c[:, h:h + 1]
            grow = gc_t[h:h + 1, :]
            decay = jnp.where(causal, jnp.exp(jnp.where(causal, gcol - grow, 0.0)), 0.0)
            kb = k * beta
            a_mat = jnp.where(strict, _mm(kb, k, _NT) * decay, 0.0)
            t_mat = _inv_unit_lower(a_mat, 5)
            eg = jnp.exp(gcol)
            rhs = jnp.concatenate([v * beta, kb * eg], axis=1)
            sol = _mm(t_mat, rhs)
            u_c = sol[:, :HEAD_D]
            w_c = sol[:, HEAD_D:]
            attn = _mm(q, k, _NT) * decay
            s_old = s_ref[h]
            v_new = u_c - _mm(w_c, s_old)
            o = _mm(q * eg, s_old) + _mm(attn, v_new)
            glast = gc_c[CHUNK - 1:CHUNK, h:h + 1]
            k_dec = k * jnp.exp(glast - gcol)
            s_ref[h] = s_old * jnp.exp(glast) + _mm(k_dec, v_new, _TN)
            z = x[rows, 3 * GROUP_W + h * HEAD_D:3 * GROUP_W + (h + 1) * HEAD_D]
            outs.append(_rms(o, norm_w) * _silu(z))
            del hs
        o_ref[0, rows, :] = jnp.concatenate(outs, axis=1)


def _gdn(u, conv_w, prm, *, n_chunks=1):
    bsz, seq, _ = u.shape
    tt = n_chunks * CHUNK
    return pl.pallas_call(
        functools.partial(_gdn_kernel, n_chunks=n_chunks),
        out_shape=jax.ShapeDtypeStruct((bsz, seq, GROUP_W), F32),
        grid=(bsz, seq // tt),
        in_specs=[
            pl.BlockSpec((1, tt, 4 * GROUP_W), lambda b, t: (b, t, GDN_BLK)),
            pl.BlockSpec((1, tt, LANES), lambda b, t: (b, t, GATES_BLK)),
            pl.BlockSpec((CONV_K, 3 * GROUP_W), lambda b, t: (0, 0)),
            pl.BlockSpec((8, LANES), lambda b, t: (0, 0)),
        ],
        out_specs=pl.BlockSpec((1, tt, GROUP_W), lambda b, t: (b, t, 0)),
        scratch_shapes=[
            pltpu.VMEM((HALO + tt, 3 * GROUP_W), F32),
            pltpu.VMEM((N_HEADS, HEAD_D, HEAD_D), F32),
        ],
        compiler_params=pltpu.CompilerParams(
            dimension_semantics=("parallel", "arbitrary"), vmem_limit_bytes=VMEM_LIMIT),
        name="gdn",
    )(u, u, conv_w, prm)


def _mlstm_kernel(u_ref, gt_ref, cw_ref, prm_ref, o_ref, buf, c_ref, n_ref, m_ref, *, n_chunks):
    @pl.when(pl.program_id(1) == 0)
    def _():
        buf[0:HALO, :] = jnp.zeros((HALO, 2 * GROUP_W), F32)
        c_ref[...] = jnp.zeros_like(c_ref)
        n_ref[...] = jnp.zeros_like(n_ref)
        m_ref[...] = jnp.zeros_like(m_ref)

    x = u_ref[0]
    qk = _silu(_causal_conv(buf, x, 2 * GROUP_W, cw_ref[...]))
    gates = gt_ref[0]
    prm = prm_ref[...]
    pre = gates + prm[0:1, :]
    logf_all = -_softplus(-pre)
    b_all = _chunk_cumsum(logf_all)

    ri = lax.broadcasted_iota(jnp.int32, (CHUNK, CHUNK), 0)
    ci = lax.broadcasted_iota(jnp.int32, (CHUNK, CHUNK), 1)
    causal = ri >= ci

    for c in range(n_chunks):
        rows = slice(c * CHUNK, (c + 1) * CHUNK)
        b_c = b_all[rows, :]
        i_c = pre[rows, :]
        b_t = b_c.T
        i_t = i_c.T
        outs = []
        for h in range(N_HEADS):
            q = qk[rows, h * HEAD_D:(h + 1) * HEAD_D]
            k = qk[rows, GROUP_W + h * HEAD_D:GROUP_W + (h + 1) * HEAD_D] * (HEAD_D ** -0.5)
            v = x[rows, 2 * GROUP_W + h * HEAD_D:2 * GROUP_W + (h + 1) * HEAD_D]
            li = 2 * N_HEADS + h
            lf = 3 * N_HEADS + h
            bcol = b_c[:, lf:lf + 1]
            icol = i_c[:, li:li + 1]
            brow = b_t[lf:lf + 1, :]
            irow = i_t[li:li + 1, :]
            m_old = m_ref[h][0:1, 0:1]
            d = jnp.where(causal, bcol - brow + irow, -jnp.inf)
            inter = bcol + m_old
            m_t = jnp.maximum(inter, jnp.max(d, axis=-1, keepdims=True))
            s = _mm(q, k, _NT) * jnp.exp(d - m_t)
            a = jnp.exp(inter - m_t)
            c_old = c_ref[h]
            n_old = n_ref[h][0:1, :]
            num = a * _mm(q, c_old) + _mm(s, v)
            den = a * jnp.sum(q * n_old, axis=-1, keepdims=True) + jnp.sum(s, axis=-1, keepdims=True)
            h_t = num / jnp.maximum(jnp.abs(den), jnp.exp(-m_t))
            g = bcol[CHUNK - 1:CHUNK, :]
            w_log = g - bcol + icol
            m_new = jnp.maximum(g + m_old, jnp.max(w_log, axis=0, keepdims=True))
            scale_old = jnp.exp(g + m_old - m_new)
            kw = k * jnp.exp(w_log - m_new)
            c_ref[h] = scale_old * c_old + _mm(kw, v, _TN)
            n_ref[h] = jnp.broadcast_to(scale_old * n_old + jnp.sum(kw, axis=0, keepdims=True), (8, HEAD_D))
            m_ref[h] = jnp.broadcast_to(m_new, (8, HEAD_D))
            o_gate = x[rows, 3 * GROUP_W + h * HEAD_D:3 * GROUP_W + (h + 1) * HEAD_D]
            outs.append(_rms(_sigmoid(o_gate) * h_t, prm[1 + h:2 + h, :]))
        o_ref[0, rows, :] = jnp.concatenate(outs, axis=1)


def _mlstm(u, conv_w, prm, *, n_chunks=1):
    bsz, seq, _ = u.shape
    tt = n_chunks * CHUNK
    return pl.pallas_call(
        functools.partial(_mlstm_kernel, n_chunks=n_chunks),
        out_shape=jax.ShapeDtypeStruct((bsz, seq, GROUP_W), F32),
        grid=(bsz, seq // tt),
        in_specs=[
            pl.BlockSpec((1, tt, 4 * GROUP_W), lambda b, t: (b, t, MLSTM_BLK)),
            pl.BlockSpec((1, tt, LANES), lambda b, t: (b, t, GATES_BLK)),
            pl.BlockSpec((CONV_K, 2 * GROUP_W), lambda b, t: (0, 0)),
            pl.BlockSpec((8, LANES), lambda b, t: (0, 0)),
        ],
        out_specs=pl.BlockSpec((1, tt, GROUP_W), lambda b, t: (b, t, 0)),
        scratch_shapes=[
            pltpu.VMEM((HALO + tt, 2 * GROUP_W), F32),
            pltpu.VMEM((N_HEADS, HEAD_D, HEAD_D), F32),
            pltpu.VMEM((N_HEADS, 8, HEAD_D), F32),
            pltpu.VMEM((N_HEADS, 8, HEAD_D), F32),
        ],
        compiler_params=pltpu.CompilerParams(
            dimension_semantics=("parallel", "arbitrary"), vmem_limit_bytes=VMEM_LIMIT),
        name="mlstm",
    )(u, u, conv_w, prm)


def _rglru_kernel(u_ref, cw_ref, prm_ref, wa_ref, wx_ref, o_ref, buf, h_ref):
    @pl.when(pl.program_id(1) == 0)
    def _():
        buf[0:HALO, :] = jnp.zeros((HALO, GROUP_W), F32)
        h_ref[...] = jnp.zeros_like(h_ref)

    x = u_ref[0]
    tt = x.shape[0]
    prm = prm_ref[...]
    xb = _causal_conv(buf, x, GROUP_W, cw_ref[...]) + prm[0:1, :]
    ra = []
    rx = []
    for n in range(N_HEADS):
        blk = xb[:, n * HEAD_D:(n + 1) * HEAD_D]
        ra.append(_mm(blk, wa_ref[n]))
        rx.append(_mm(blk, wx_ref[n]))
    r = _sigmoid(jnp.concatenate(ra, axis=1) + prm[1:2, :])
    i = _sigmoid(jnp.concatenate(rx, axis=1) + prm[2:3, :])
    log_a = -RGLRU_C * r * _softplus(-prm[3:4, :])
    a = jnp.exp(log_a)
    th = jnp.tanh(log_a)
    b = jnp.sqrt(-2.0 * th / (1.0 - th)) * (i * xb)
    row = lax.broadcasted_iota(jnp.int32, (tt, GROUP_W), 0)
    d = 1
    while d < tt:
        keep = row >= d
        b = jnp.where(keep, a * pltpu.roll(b, d, 0) + b, b)
        a = jnp.where(keep, a * pltpu.roll(a, d, 0), a)
        d *= 2
    h = b + a * h_ref[0:1, :]
    h_ref[...] = jnp.broadcast_to(h[tt - 1:tt, :], (8, GROUP_W))
    gate = x[:, GROUP_W:]
    gelu = 0.5 * gate * (1.0 + jnp.tanh(0.7978845608028654 * (gate + 0.044715 * gate * gate * gate)))
    o_ref[0] = h * gelu


def _rglru(u, conv_w, prm, w_a, w_x, *, tt=256):
    bsz, seq, _ = u.shape
    return pl.pallas_call(
        _rglru_kernel,
        out_shape=jax.ShapeDtypeStruct((bsz, seq, GROUP_W), F32),
        grid=(bsz, seq // tt),
        in_specs=[
            pl.BlockSpec((1, tt, 2 * GROUP_W), lambda b, t: (b, t, RGLRU_BLK)),
            pl.BlockSpec((CONV_K, GROUP_W), lambda b, t: (0, 0)),
            pl.BlockSpec((8, GROUP_W), lambda b, t: (0, 0)),
            pl.BlockSpec((N_HEADS, HEAD_D, HEAD_D), lambda b, t: (0, 0, 0)),
            pl.BlockSpec((N_HEADS, HEAD_D, HEAD_D), lambda b, t: (0, 0, 0)),
        ],
        out_specs=pl.BlockSpec((1, tt, GROUP_W), lambda b, t: (b, t, 0)),
        scratch_shapes=[
            pltpu.VMEM((HALO + tt, GROUP_W), F32),
            pltpu.VMEM((8, GROUP_W), F32),
        ],
        compiler_params=pltpu.CompilerParams(
            dimension_semantics=("parallel", "arbitrary"), vmem_limit_bytes=VMEM_LIMIT),
        name="rglru",
    )(u, conv_w, prm, w_a, w_x)


def _pair_sum(x, lo_mask):
    s_lo = jnp.sum(jnp.where(lo_mask, x, 0.0), axis=-1, keepdims=True)
    s_hi = jnp.sum(jnp.where(lo_mask, 0.0, x), axis=-1, keepdims=True)
    return jnp.where(lo_mask, s_lo, s_hi)


def _stack_heads(x, lo_mask):
    return jnp.concatenate([jnp.where(lo_mask, x, 0.0), jnp.where(lo_mask, 0.0, x)], axis=0)


def _rwkv_kernel(u_ref, mu_ref, prm_ref, wlo_ref, g2_ref, o_ref, buf, s_ref, *, n_chunks):
    @pl.when(pl.program_id(1) == 0)
    def _():
        buf[0:HALO, :] = jnp.zeros((HALO, RWKV_USED), F32)
        s_ref[...] = jnp.zeros_like(s_ref)

    x = u_ref[0][:, :RWKV_USED]
    tt = x.shape[0]
    buf[HALO:HALO + tt, :] = x
    x_prev = buf[HALO - 1:HALO - 1 + tt, :]
    buf[0:HALO, :] = x[tt - HALO:tt, :]
    xs = x + (x_prev - x) * mu_ref[...]
    prm = prm_ref[...]
    r_all = xs[:, 0:GROUP_W]
    k_all = xs[:, GROUP_W:2 * GROUP_W]
    v_all = xs[:, 2 * GROUP_W:3 * GROUP_W]
    lora = xs[:, 3 * GROUP_W:3 * GROUP_W + LANES]
    lane = lax.broadcasted_iota(jnp.int32, lora.shape, 1)
    lora = jnp.where(lane < RWKV_N, jnp.tanh(lora), lora)
    wa = _mm(lora, wlo_ref[...])
    w_raw = -_softplus(-(prm[0:1, :] + wa[:, :GROUP_W])) - 0.5
    lw_all = -jnp.exp(w_raw)
    a_all = _sigmoid(prm[1:2, :] + wa[:, GROUP_W:])
    g_all = _mm(_sigmoid(xs[:, 3 * GROUP_W + LANES:3 * GROUP_W + 2 * LANES]), g2_ref[...])
    kk_all = k_all * prm[2:3, :]
    k2_all = k_all * (1.0 + (a_all - 1.0) * prm[3:4, :])
    lc_all = _chunk_cumsum(lw_all)

    lo_mask = lax.broadcasted_iota(jnp.int32, (CHUNK, LANES), 1) < RWKV_N
    n2 = 2 * CHUNK
    ri = lax.broadcasted_iota(jnp.int32, (n2, n2), 0)
    ci = lax.broadcasted_iota(jnp.int32, (n2, n2), 1)
    same = (ri // CHUNK) == (ci // CHUNK)
    strict = same & ((ci % CHUNK) < (ri % CHUNK))
    incl = same & ((ci % CHUNK) <= (ri % CHUNK))

    for c in range(n_chunks):
        rows = slice(c * CHUNK, (c + 1) * CHUNK)
        outs = []
        for p in range(GROUP_W // LANES):
            cols = slice(p * LANES, (p + 1) * LANES)
            r = r_all[rows, cols]
            v = v_all[rows, cols]
            k2 = k2_all[rows, cols]
            kk = kk_all[rows, cols]
            kk = kk * lax.rsqrt(_pair_sum(kk * kk, lo_mask) + 1e-6)
            a_sig = a_all[rows, cols]
            lw = lw_all[rows, cols]
            lc = lc_all[rows, cols]
            w_incl = jnp.exp(lc)
            w_inv = jnp.exp(-lc)
            a_hat = _stack_heads(-kk * jnp.exp(lc - lw), lo_mask)
            b_hat = _stack_heads(kk * a_sig * w_inv, lo_mask)
            k_hat = _stack_heads(k2 * w_inv, lo_mask)
            r_hat = _stack_heads(r * w_incl, lo_mask)
            v_st = _stack_heads(v, lo_mask)
            s_old = s_ref[p]
            a_ab = jnp.where(strict, _mm(a_hat, b_hat, _NT), 0.0)
            a_ak = jnp.where(strict, _mm(a_hat, k_hat, _NT), 0.0)
            t_mat = _inv_unit_lower(-a_ab, 5)
            pv = _mm(t_mat, _mm(a_hat, s_old, _NT) + _mm(a_ak, v_st))
            m_rb = jnp.where(incl, _mm(r_hat, b_hat, _NT), 0.0)
            m_rk = jnp.where(incl, _mm(r_hat, k_hat, _NT), 0.0)
            y_st = _mm(r_hat, s_old, _NT) + _mm(m_rb, pv) + _mm(m_rk, v_st)
            w_last = w_incl[CHUNK - 1:CHUNK, :]
            s_ref[p] = (s_old + _mm(pv, b_hat, _TN) + _mm(v_st, k_hat, _TN)) * w_last
            y = y_st[:CHUNK, :] + y_st[CHUNK:, :]
            mu_y = _pair_sum(y, lo_mask) * (1.0 / RWKV_N)
            yc = y - mu_y
            var = _pair_sum(yc * yc, lo_mask) * (1.0 / RWKV_N)
            y = yc * lax.rsqrt(var + RWKV_LN_EPS) * prm[5:6, cols] + prm[6:7, cols]
            y = y + _pair_sum(r * k2 * prm[4:5, cols], lo_mask) * v
            outs.append(y * g_all[rows, cols])
        o_ref[0, rows, :] = jnp.concatenate(outs, axis=1)


def _rwkv(u, mu, prm, w_lora, g2, *, n_chunks=1):
    bsz, seq, _ = u.shape
    tt = n_chunks * CHUNK
    return pl.pallas_call(
        functools.partial(_rwkv_kernel, n_chunks=n_chunks),
        out_shape=jax.ShapeDtypeStruct((bsz, seq, GROUP_W), F32),
        grid=(bsz, seq // tt),
        in_specs=[
            pl.BlockSpec((1, tt, 4 * GROUP_W), lambda b, t: (b, t, RWKV_BLK)),
            pl.BlockSpec((1, RWKV_USED), lambda b, t: (0, 0)),
            pl.BlockSpec((8, GROUP_W), lambda b, t: (0, 0)),
            pl.BlockSpec((LANES, 2 * GROUP_W), lambda b, t: (0, 0)),
            pl.BlockSpec((LANES, GROUP_W), lambda b, t: (0, 0)),
        ],
        out_specs=pl.BlockSpec((1, tt, GROUP_W), lambda b, t: (b, t, 0)),
        scratch_shapes=[
            pltpu.VMEM((HALO + tt, RWKV_USED), F32),
            pltpu.VMEM((GROUP_W // LANES, LANES, LANES), F32),
        ],
        compiler_params=pltpu.CompilerParams(
            dimension_semantics=("parallel", "arbitrary"), vmem_limit_bytes=VMEM_LIMIT),
        name="rwkv",
    )(u, mu, prm, w_lora, g2)


def _outproj_kernel(x_ref, y0_ref, y1_ref, y2_ref, y3_ref, w_ref, o_ref):
    acc = x_ref[...]
    for g, y_ref in enumerate((y0_ref, y1_ref, y2_ref, y3_ref)):
        acc = acc + jnp.dot(y_ref[...].astype(BF16), w_ref[g * GROUP_W:(g + 1) * GROUP_W, :],
                            preferred_element_type=F32)
    o_ref[...] = acc


def _outproj(x2d, ys, w_out, *, tm=512, tn=1024):
    n_tok = x2d.shape[0]
    y_spec = pl.BlockSpec((tm, GROUP_W), lambda i, j: (i, 0))
    return pl.pallas_call(
        _outproj_kernel,
        out_shape=jax.ShapeDtypeStruct((n_tok, D_MODEL), F32),
        grid=(n_tok // tm, D_MODEL // tn),
        in_specs=[pl.BlockSpec((tm, tn), lambda i, j: (i, j)), y_spec, y_spec, y_spec, y_spec,
                  pl.BlockSpec((D_MODEL, tn), lambda i, j: (0, j))],
        out_specs=pl.BlockSpec((tm, tn), lambda i, j: (i, j)),
        compiler_params=pltpu.CompilerParams(
            dimension_semantics=("parallel", "arbitrary"), vmem_limit_bytes=VMEM_LIMIT),
        name="outproj",
    )(x2d, *ys, w_out)


def _ffn_kernel(x_ref, nw_ref, wg_ref, wu_ref, wd_ref, fw_ref, o_ref, n_scr, *, n_f, final):
    f = pl.program_id(1)

    @pl.when(f == 0)
    def _():
        x = x_ref[...]
        n_scr[...] = _rms(x, nw_ref[...]).astype(BF16)
        o_ref[...] = x

    n = n_scr[...]
    gate = jnp.dot(n, wg_ref[...], preferred_element_type=F32)
    up = jnp.dot(n, wu_ref[...], preferred_element_type=F32)
    hid = (_silu(gate) * up).astype(BF16)
    o_ref[...] += jnp.dot(hid, wd_ref[...], preferred_element_type=F32)

    if final:
        @pl.when(f == n_f - 1)
        def _():
            o_ref[...] = _rms(o_ref[...], fw_ref[...])


def _ffn(x2d, norm_w, w_gate, w_up, w_down, final_w, *, final, tm=512, tf=512):
    n_tok = x2d.shape[0]
    n_f = FFN_HIDDEN // tf
    return pl.pallas_call(
        functools.partial(_ffn_kernel, n_f=n_f, final=final),
        out_shape=jax.ShapeDtypeStruct((n_tok, D_MODEL), F32),
        grid=(n_tok // tm, n_f),
        in_specs=[
            pl.BlockSpec((tm, D_MODEL), lambda i, f: (i, 0)),
            pl.BlockSpec((1, D_MODEL), lambda i, f: (0, 0)),
            pl.BlockSpec((D_MODEL, tf), lambda i, f: (0, f)),
            pl.BlockSpec((D_MODEL, tf), lambda i, f: (0, f)),
            pl.BlockSpec((tf, D_MODEL), lambda i, f: (f, 0)),
            pl.BlockSpec((1, D_MODEL), lambda i, f: (0, 0)),
        ],
        out_specs=pl.BlockSpec((tm, D_MODEL), lambda i, f: (i, 0)),
        scratch_shapes=[pltpu.VMEM((tm, D_MODEL), BF16)],
        compiler_params=pltpu.CompilerParams(
            dimension_semantics=("parallel", "arbitrary"), vmem_limit_bytes=VMEM_LIMIT),
        name="ffn_final" if final else "ffn",
    )(x2d, norm_w.reshape(1, D_MODEL), w_gate, w_up, w_down, final_w.reshape(1, D_MODEL))


def _pad_lanes(v, offset, width):
    return jnp.zeros((width,), F32).at[offset:offset + v.shape[0]].set(v.astype(F32))


def _rows(rows, width):
    out = jnp.zeros((8, width), F32)
    for i, r in enumerate(rows):
        out = out.at[i, :].set(r)
    return out


def _perm_w_in(w):
    d = w.shape[0]
    gdn_main, gdn_ab = w[:, 0:2048], w[:, 2048:2056]
    ml_main, ml_if = w[:, 2056:4104], w[:, 4104:4112]
    rg = w[:, 4112:5136]
    rw = w[:, 5136:6928]
    gates = jnp.concatenate([gdn_ab, ml_if, jnp.zeros((d, LANES - 16), w.dtype)], axis=1)
    pad = jnp.zeros((d, LANES), w.dtype)
    return jnp.concatenate([gdn_main, ml_main, rw, gates, pad, rg], axis=1).astype(BF16)


def kernel(x, attn_norm, w_in, gdn_conv, gdn_a_log, gdn_dt_bias, gdn_norm, mlstm_conv, mlstm_b_i, mlstm_b_f, mlstm_norm, rglru_conv, rglru_conv_b, rglru_w_a, rglru_b_a, rglru_w_x, rglru_b_x, rglru_lambda, rwkv_mu, rwkv_w0, rwkv_w2, rwkv_a0, rwkv_a2, rwkv_g2, rwkv_k_k, rwkv_k_a, rwkv_r_k, rwkv_ln_w, rwkv_ln_b, w_out, ffn_norm, ffn_w_gate, ffn_w_up, ffn_w_down, final_norm):
    bsz, seq, d = x.shape
    depth = w_in.shape[0]
    x2d = x.reshape(bsz * seq, d)
    for l in range(depth):
        u = _inproj(x2d, attn_norm[l], _perm_w_in(w_in[l])).reshape(bsz, seq, U_COLS)

        gdn_prm = _rows([_pad_lanes(gdn_a_log[l], 0, LANES), _pad_lanes(gdn_dt_bias[l], 0, LANES),
                         gdn_norm[l]], LANES)
        y_gdn = _gdn(u, gdn_conv[l], gdn_prm)

        ml_bias = _pad_lanes(mlstm_b_i[l], 2 * N_HEADS, LANES) + _pad_lanes(mlstm_b_f[l], 3 * N_HEADS, LANES)
        ml_prm = _rows([ml_bias] + [mlstm_norm[l, h] for h in range(N_HEADS)], LANES)
        y_ml = _mlstm(u, mlstm_conv[l], ml_prm)

        rg_prm = _rows([rglru_conv_b[l], rglru_b_a[l], rglru_b_x[l], rglru_lambda[l]], GROUP_W)
        y_rg = _rglru(u, rglru_conv[l], rg_prm, rglru_w_a[l].astype(BF16), rglru_w_x[l].astype(BF16))

        rw_prm = _rows([rwkv_w0[l], rwkv_a0[l], rwkv_k_k[l], rwkv_k_a[l], rwkv_r_k[l].reshape(-1),
                        rwkv_ln_w[l].reshape(-1), rwkv_ln_b[l].reshape(-1)], GROUP_W)
        zeros_lora = jnp.zeros((RWKV_N, GROUP_W), F32)
        w_lora = jnp.concatenate([
            jnp.concatenate([rwkv_w2[l], zeros_lora], axis=1),
            jnp.concatenate([zeros_lora, rwkv_a2[l]], axis=1)], axis=0).astype(BF16)
        y_rw = _rwkv(u, rwkv_mu[l].reshape(1, RWKV_USED), rw_prm, w_lora, rwkv_g2[l].astype(BF16))

        ys = [y.reshape(bsz * seq, GROUP_W) for y in (y_gdn, y_ml, y_rg, y_rw)]
        x2d = _outproj(x2d, ys, w_out[l].astype(BF16))
        x2d = _ffn(x2d, ffn_norm[l], ffn_w_gate[l].astype(BF16), ffn_w_up[l].astype(BF16),
                   ffn_w_down[l].astype(BF16), final_norm, final=(l == depth - 1))
    return x2d.reshape(bsz, seq, d)
```

```python
import functools

import jax
import jax.numpy as jnp
from jax import lax
from jax.experimental import pallas as pl
from jax.experimental.pallas import tpu as pltpu

F32 = jnp.float32
BF16 = jnp.bfloat16

D_MODEL = 2048
GROUP_W = 512
HEAD_D = 128
N_HEADS = 4
RWKV_N = 64
N_PAIRS = GROUP_W // (2 * RWKV_N)
CONV_K = 4
CHUNK = 64
FFN_HIDDEN = 5632
NORM_EPS = 1e-6
RWKV_LN_EPS = 64e-5
RGLRU_C = 8.0
LANES = 128
HALO = 8
INV_LEVELS = 5

U_COLS = 7168
GDN_BLK = 0
MLSTM_BLK = 1
RWKV_BLK = 2
GATES_BLK = 46
RGLRU_BLK = 6
RWKV_USED = 1792

VMEM_LIMIT = 56 * 1024 * 1024

_NN = (((1,), (0,)), ((), ()))
_NT = (((1,), (1,)), ((), ()))
_TN = (((0,), (0,)), ((), ()))


def _dot(a, b, dims=_NN):
    return lax.dot_general(a, b, dims, preferred_element_type=F32)


def _mm(a, b, dims=_NN):
    return _dot(a.astype(BF16), b.astype(BF16), dims)


def _split(a):
    hi = a.astype(BF16)
    return hi, (a - hi.astype(F32)).astype(BF16)


def _mm3(a, b):
    return _dot(a[0], b[0]) + (_dot(a[0], b[1]) + _dot(a[1], b[0]))


def _sigmoid(x):
    return 1.0 / (1.0 + jnp.exp(-x))


def _softplus(x):
    return jnp.maximum(x, 0.0) + jnp.log1p(jnp.exp(-jnp.abs(x)))


def _silu(x):
    return x * _sigmoid(x)


def _rms(x, w, eps=NORM_EPS):
    return x * lax.rsqrt(jnp.mean(x * x, axis=-1, keepdims=True) + eps) * w


def _chunk_cumsum(x):
    row = lax.broadcasted_iota(jnp.int32, x.shape, 0) % CHUNK
    d = 1
    while d < CHUNK:
        x = x + jnp.where(row >= d, pltpu.roll(x, d, 0), 0.0)
        d *= 2
    return x


def _inv_unit_lower(a_list):
    n = a_list[0].shape[0]
    eye = (lax.broadcasted_iota(jnp.int32, (n, n), 0) == lax.broadcasted_iota(jnp.int32, (n, n), 1)).astype(F32)
    p = [-a for a in a_list]
    inv = [eye + x for x in p]
    ps = [_split(x) for x in p]
    for _ in range(INV_LEVELS):
        p = [_mm3(s, s) for s in ps]
        ps = [_split(x) for x in p]
        inv_s = [_split(x) for x in inv]
        inv = [x + _mm3(xs, s) for x, xs, s in zip(inv, inv_s, ps)]
    return inv


def _causal_conv(buf, x, conv_w):
    tt = x.shape[0]
    buf[HALO:HALO + tt, :] = x
    acc = None
    for j in range(CONV_K):
        lo = HALO - (CONV_K - 1) + j
        term = conv_w[j:j + 1, :] * buf[lo:lo + tt, :]
        acc = term if acc is None else acc + term
    buf[0:HALO, :] = x[tt - HALO:tt, :]
    return acc


def _brows(b):
    return slice(b * CHUNK, (b + 1) * CHUNK)


def _hcols(h, base=0):
    return slice(base + h * HEAD_D, base + (h + 1) * HEAD_D)


def _inproj_kernel(x_ref, nw_ref, w_ref, o_ref, n_scr):
    @pl.when(pl.program_id(1) == 0)
    def _():
        n_scr[...] = _rms(x_ref[...], nw_ref[...]).astype(BF16)

    o_ref[...] = jnp.dot(n_scr[...], w_ref[...], preferred_element_type=F32)


def _inproj(x2d, norm_w, w_perm, *, tm=512, tn=1024):
    n_tok = x2d.shape[0]
    return pl.pallas_call(
        _inproj_kernel,
        out_shape=jax.ShapeDtypeStruct((n_tok, U_COLS), F32),
        grid=(n_tok // tm, U_COLS // tn),
        in_specs=[
            pl.BlockSpec((tm, D_MODEL), lambda i, j: (i, 0)),
            pl.BlockSpec((1, D_MODEL), lambda i, j: (0, 0)),
            pl.BlockSpec((D_MODEL, tn), lambda i, j: (0, j)),
        ],
        out_specs=pl.BlockSpec((tm, tn), lambda i, j: (i, j)),
        scratch_shapes=[pltpu.VMEM((tm, D_MODEL), BF16)],
        compiler_params=pltpu.CompilerParams(
            dimension_semantics=("parallel", "arbitrary"), vmem_limit_bytes=VMEM_LIMIT),
        name="inproj",
    )(x2d, norm_w.reshape(1, D_MODEL), w_perm)


def _gdn_kernel(u_ref, gt_ref, cw_ref, prm_ref, o_ref, buf, s_ref):
    nb = u_ref.shape[0]

    @pl.when(pl.program_id(0) == 0)
    def _():
        buf[:, 0:HALO, :] = jnp.zeros((nb, HALO, 3 * GROUP_W), F32)
        s_ref[...] = jnp.zeros_like(s_ref)

    cw = cw_ref[...]
    qkv = _silu(jnp.concatenate(
        [_causal_conv(buf.at[b], u_ref[b, :, 0:3 * GROUP_W], cw) for b in range(nb)], axis=0))
    gates = gt_ref[...].reshape(nb * CHUNK, LANES)
    prm = prm_ref[...]
    g_all = -jnp.exp(prm[0:1, :]) * _softplus(gates + prm[1:2, :])
    beta_all = _sigmoid(gates)
    gc_all = _chunk_cumsum(g_all)
    norm_w = prm[2:3, :]
    gc_t = [gc_all[_brows(b), :].T for b in range(nb)]

    qn, kn = [], []
    for h in range(N_HEADS):
        q = qkv[:, _hcols(h)]
        k = qkv[:, _hcols(h, GROUP_W)]
        qn.append(q * (lax.rsqrt(jnp.sum(q * q, axis=-1, keepdims=True) + 1e-6) * (HEAD_D ** -0.5)))
        kn.append(k * lax.rsqrt(jnp.sum(k * k, axis=-1, keepdims=True) + 1e-6))

    ri = lax.broadcasted_iota(jnp.int32, (CHUNK, CHUNK), 0)
    ci = lax.broadcasted_iota(jnp.int32, (CHUNK, CHUNK), 1)
    causal = ri >= ci
    strict = ri > ci

    probs = [(b, h) for b in range(nb) for h in range(N_HEADS)]
    q = [qn[h][_brows(b), :] for b, h in probs]
    k = [kn[h][_brows(b), :] for b, h in probs]
    v = [qkv[_brows(b), _hcols(h, 2 * GROUP_W)] for b, h in probs]
    beta = [beta_all[_brows(b), N_HEADS + h:N_HEADS + h + 1] for b, h in probs]
    gcol = [gc_all[_brows(b), h:h + 1] for b, h in probs]
    grow = [gc_t[b][h:h + 1, :] for b, h in probs]
    decay = [jnp.where(causal, jnp.exp(jnp.where(causal, gc - gr, 0.0)), 0.0) for gc, gr in zip(gcol, grow)]
    kb = [x * y for x, y in zip(k, beta)]
    raw = [_mm(jnp.concatenate([x, y], axis=0), z, _NT) for x, y, z in zip(kb, q, k)]
    a_mat = [jnp.where(strict, r[:CHUNK] * d, 0.0) for r, d in zip(raw, decay)]
    attn = [r[CHUNK:] * d for r, d in zip(raw, decay)]
    t_mat = _inv_unit_lower(a_mat)
    eg = [jnp.exp(x) for x in gcol]
    sol = [_mm(t, jnp.concatenate([x * y, z * e], axis=1))
           for t, x, y, z, e in zip(t_mat, v, beta, kb, eg)]
    s_old = [s_ref[b, h] for b, h in probs]
    ws = [_mm(jnp.concatenate([x[:, HEAD_D:], y * e], axis=0), s)
          for x, y, e, s in zip(sol, q, eg, s_old)]
    v_new = [x[:, :HEAD_D] - y[:CHUNK] for x, y in zip(sol, ws)]
    o = [y[CHUNK:] + _mm(a, x) for y, a, x in zip(ws, attn, v_new)]
    glast = [x[CHUNK - 1:CHUNK, :] for x in gcol]
    s_new = [s * jnp.exp(gl) + _mm(x * jnp.exp(gl - gc), vn, _TN)
             for s, gl, x, gc, vn in zip(s_old, glast, k, gcol, v_new)]
    for (b, h), s in zip(probs, s_new):
        s_ref[b, h] = s
    for (b, h), x in zip(probs, o):
        z = u_ref[b, :, _hcols(h, 3 * GROUP_W)]
        o_ref[b, :, _hcols(h)] = _rms(x, norm_w) * _silu(z)


def _gdn(u, conv_w, prm):
    bsz, seq, _ = u.shape
    return pl.pallas_call(
        _gdn_kernel,
        out_shape=jax.ShapeDtypeStruct((bsz, seq, GROUP_W), F32),
        grid=(seq // CHUNK,),
        in_specs=[
            pl.BlockSpec((bsz, CHUNK, 4 * GROUP_W), lambda t: (0, t, GDN_BLK)),
            pl.BlockSpec((bsz, CHUNK, LANES), lambda t: (0, t, GATES_BLK)),
            pl.BlockSpec((CONV_K, 3 * GROUP_W), lambda t: (0, 0)),
            pl.BlockSpec((8, LANES), lambda t: (0, 0)),
        ],
        out_specs=pl.BlockSpec((bsz, CHUNK, GROUP_W), lambda t: (0, t, 0)),
        scratch_shapes=[
            pltpu.VMEM((bsz, HALO + CHUNK, 3 * GROUP_W), F32),
            pltpu.VMEM((bsz, N_HEADS, HEAD_D, HEAD_D), F32),
        ],
        compiler_params=pltpu.CompilerParams(
            dimension_semantics=("arbitrary",), vmem_limit_bytes=VMEM_LIMIT),
        name="gdn",
    )(u, u, conv_w, prm)


def _mlstm_kernel(u_ref, gt_ref, cw_ref, prm_ref, o_ref, buf, c_ref, n_ref, m_ref):
    nb = u_ref.shape[0]

    @pl.when(pl.program_id(0) == 0)
    def _():
        buf[:, 0:HALO, :] = jnp.zeros((nb, HALO, 2 * GROUP_W), F32)
        c_ref[...] = jnp.zeros_like(c_ref)
        n_ref[...] = jnp.zeros_like(n_ref)
        m_ref[...] = jnp.zeros_like(m_ref)

    cw = cw_ref[...]
    qk = _silu(jnp.concatenate(
        [_causal_conv(buf.at[b], u_ref[b, :, 0:2 * GROUP_W], cw) for b in range(nb)], axis=0))
    gates = gt_ref[...].reshape(nb * CHUNK, LANES)
    prm = prm_ref[...]
    pre = gates + prm[0:1, :]
    b_all = _chunk_cumsum(-_softplus(-pre))
    b_tr = [b_all[_brows(b), :].T for b in range(nb)]
    i_tr = [pre[_brows(b), :].T for b in range(nb)]

    ri = lax.broadcasted_iota(jnp.int32, (CHUNK, CHUNK), 0)
    ci = lax.broadcasted_iota(jnp.int32, (CHUNK, CHUNK), 1)
    causal = ri >= ci

    probs = [(b, h) for b in range(nb) for h in range(N_HEADS)]
    li = [2 * N_HEADS + h for _, h in probs]
    lf = [3 * N_HEADS + h for _, h in probs]
    q = [qk[_brows(b), _hcols(h)] for b, h in probs]
    k = [qk[_brows(b), _hcols(h, GROUP_W)] * (HEAD_D ** -0.5) for b, h in probs]
    v = [u_ref[b, :, _hcols(h, 2 * GROUP_W)] for b, h in probs]
    bcol = [b_all[_brows(b), l:l + 1] for (b, _), l in zip(probs, lf)]
    icol = [pre[_brows(b), l:l + 1] for (b, _), l in zip(probs, li)]
    brow = [b_tr[b][l:l + 1, :] for (b, _), l in zip(probs, lf)]
    irow = [i_tr[b][l:l + 1, :] for (b, _), l in zip(probs, li)]
    m_old = [m_ref[b, h][0:1, 0:1] for b, h in probs]
    c_old = [c_ref[b, h] for b, h in probs]
    n_old = [n_ref[b, h][0:1, :] for b, h in probs]

    d = [jnp.where(causal, bc - br + ir, -jnp.inf) for bc, br, ir in zip(bcol, brow, irow)]
    inter = [bc + m for bc, m in zip(bcol, m_old)]
    m_t = [jnp.maximum(x, jnp.max(y, axis=-1, keepdims=True)) for x, y in zip(inter, d)]
    qk_raw = [_mm(x, y, _NT) for x, y in zip(q, k)]
    s = [r * jnp.exp(x - m) for r, x, m in zip(qk_raw, d, m_t)]
    a = [jnp.exp(x - m) for x, m in zip(inter, m_t)]
    qc = [_mm(x, c) for x, c in zip(q, c_old)]
    sv = [_mm(x, y) for x, y in zip(s, v)]
    num = [x * y + z for x, y, z in zip(a, qc, sv)]
    den = [x * jnp.sum(y * n, axis=-1, keepdims=True) + jnp.sum(z, axis=-1, keepdims=True)
           for x, y, n, z in zip(a, q, n_old, s)]
    h_t = [x / jnp.maximum(jnp.abs(y), jnp.exp(-m)) for x, y, m in zip(num, den, m_t)]
    g = [x[CHUNK - 1:CHUNK, :] for x in bcol]
    w_log = [x - bc + ic for x, bc, ic in zip(g, bcol, icol)]
    m_new = [jnp.maximum(x + m, jnp.max(w, axis=0, keepdims=True)) for x, m, w in zip(g, m_old, w_log)]
    scale = [jnp.exp(x + m - mn) for x, m, mn in zip(g, m_old, m_new)]
    kw = [x * jnp.exp(w - mn) for x, w, mn in zip(k, w_log, m_new)]
    c_new = [sc * c + _mm(x, y, _TN) for sc, c, x, y in zip(scale, c_old, kw, v)]
    n_new = [sc * n + jnp.sum(x, axis=0, keepdims=True) for sc, n, x in zip(scale, n_old, kw)]
    for (b, h), c, n, m in zip(probs, c_new, n_new, m_new):
        c_ref[b, h] = c
        n_ref[b, h] = jnp.broadcast_to(n, (8, HEAD_D))
        m_ref[b, h] = jnp.broadcast_to(m, (8, HEAD_D))
    for (b, h), x in zip(probs, h_t):
        o_gate = u_ref[b, :, _hcols(h, 3 * GROUP_W)]
        o_ref[b, :, _hcols(h)] = _rms(_sigmoid(o_gate) * x, prm[1 + h:2 + h, :])


def _mlstm(u, conv_w, prm):
    bsz, seq, _ = u.shape
    return pl.pallas_call(
        _mlstm_kernel,
        out_shape=jax.ShapeDtypeStruct((bsz, seq, GROUP_W), F32),
        grid=(seq // CHUNK,),
        in_specs=[
            pl.BlockSpec((bsz, CHUNK, 4 * GROUP_W), lambda t: (0, t, MLSTM_BLK)),
            pl.BlockSpec((bsz, CHUNK, LANES), lambda t: (0, t, GATES_BLK)),
            pl.BlockSpec((CONV_K, 2 * GROUP_W), lambda t: (0, 0)),
            pl.BlockSpec((8, LANES), lambda t: (0, 0)),
        ],
        out_specs=pl.BlockSpec((bsz, CHUNK, GROUP_W), lambda t: (0, t, 0)),
        scratch_shapes=[
            pltpu.VMEM((bsz, HALO + CHUNK, 2 * GROUP_W), F32),
            pltpu.VMEM((bsz, N_HEADS, HEAD_D, HEAD_D), F32),
            pltpu.VMEM((bsz, N_HEADS, 8, HEAD_D), F32),
            pltpu.VMEM((bsz, N_HEADS, 8, HEAD_D), F32),
        ],
        compiler_params=pltpu.CompilerParams(
            dimension_semantics=("arbitrary",), vmem_limit_bytes=VMEM_LIMIT),
        name="mlstm",
    )(u, u, conv_w, prm)


def _rglru_kernel(u_ref, cw_ref, prm_ref, wa_ref, wx_ref, o_ref, buf, h_ref):
    @pl.when(pl.program_id(1) == 0)
    def _():
        buf[0:HALO, :] = jnp.zeros((HALO, GROUP_W), F32)
        h_ref[...] = jnp.zeros_like(h_ref)

    tt = u_ref.shape[1]
    prm = prm_ref[...]
    xb = _causal_conv(buf, u_ref[0, :, 0:GROUP_W], cw_ref[...]) + prm[0:1, :]
    ra = []
    rx = []
    for n in range(N_HEADS):
        blk = xb[:, _hcols(n)].astype(BF16)
        ra.append(_dot(blk, wa_ref[n]))
        rx.append(_dot(blk, wx_ref[n]))
    r = _sigmoid(jnp.concatenate(ra, axis=1) + prm[1:2, :])
    i = _sigmoid(jnp.concatenate(rx, axis=1) + prm[2:3, :])
    log_a = -RGLRU_C * r * _softplus(-prm[3:4, :])
    a = jnp.exp(log_a)
    th = jnp.tanh(log_a)
    b = jnp.sqrt(-2.0 * th / (1.0 - th)) * (i * xb)
    row = lax.broadcasted_iota(jnp.int32, (tt, GROUP_W), 0)
    d = 1
    while d < tt:
        keep = row >= d
        b = jnp.where(keep, a * pltpu.roll(b, d, 0) + b, b)
        a = jnp.where(keep, a * pltpu.roll(a, d, 0), a)
        d *= 2
    h = b + a * h_ref[0:1, :]
    h_ref[...] = jnp.broadcast_to(h[tt - 1:tt, :], (8, GROUP_W))
    gate = u_ref[0, :, GROUP_W:2 * GROUP_W]
    gelu = 0.5 * gate * (1.0 + jnp.tanh(0.7978845608028654 * (gate + 0.044715 * gate * gate * gate)))
    o_ref[0] = h * gelu


def _rglru(u, conv_w, prm, w_a, w_x, *, tt=256):
    bsz, seq, _ = u.shape
    return pl.pallas_call(
        _rglru_kernel,
        out_shape=jax.ShapeDtypeStruct((bsz, seq, GROUP_W), F32),
        grid=(bsz, seq // tt),
        in_specs=[
            pl.BlockSpec((1, tt, 2 * GROUP_W), lambda b, t: (b, t, RGLRU_BLK)),
            pl.BlockSpec((CONV_K, GROUP_W), lambda b, t: (0, 0)),
            pl.BlockSpec((8, GROUP_W), lambda b, t: (0, 0)),
            pl.BlockSpec((N_HEADS, HEAD_D, HEAD_D), lambda b, t: (0, 0, 0)),
            pl.BlockSpec((N_HEADS, HEAD_D, HEAD_D), lambda b, t: (0, 0, 0)),
        ],
        out_specs=pl.BlockSpec((1, tt, GROUP_W), lambda b, t: (b, t, 0)),
        scratch_shapes=[
            pltpu.VMEM((HALO + tt, GROUP_W), F32),
            pltpu.VMEM((8, GROUP_W), F32),
        ],
        compiler_params=pltpu.CompilerParams(
            dimension_semantics=("parallel", "arbitrary"), vmem_limit_bytes=VMEM_LIMIT),
        name="rglru",
    )(u, conv_w, prm, w_a, w_x)


def _pair_sum(x, lo_mask):
    s_lo = jnp.sum(jnp.where(lo_mask, x, 0.0), axis=-1, keepdims=True)
    s_hi = jnp.sum(jnp.where(lo_mask, 0.0, x), axis=-1, keepdims=True)
    return jnp.where(lo_mask, s_lo, s_hi)


def _stack_heads(x, lo_mask):
    return jnp.concatenate([jnp.where(lo_mask, x, 0.0), jnp.where(lo_mask, 0.0, x)], axis=0)


def _rwkv_kernel(u_ref, mu_ref, prm_ref, wlo_ref, g2_ref, o_ref, buf, s_ref):
    nb = u_ref.shape[0]

    @pl.when(pl.program_id(0) == 0)
    def _():
        buf[:, 0:HALO, :] = jnp.zeros((nb, HALO, RWKV_USED), F32)
        s_ref[...] = jnp.zeros_like(s_ref)

    mu = mu_ref[...]
    xs = []
    for b in range(nb):
        x = u_ref[b, :, 0:RWKV_USED]
        buf[b, HALO:HALO + CHUNK, :] = x
        x_prev = buf[b, HALO - 1:HALO - 1 + CHUNK, :]
        buf[b, 0:HALO, :] = x[CHUNK - HALO:CHUNK, :]
        xs.append(x + (x_prev - x) * mu)
    xs = jnp.concatenate(xs, axis=0)
    prm = prm_ref[...]
    r_all = xs[:, 0:GROUP_W]
    k_all = xs[:, GROUP_W:2 * GROUP_W]
    v_all = xs[:, 2 * GROUP_W:3 * GROUP_W]
    lora = xs[:, 3 * GROUP_W:3 * GROUP_W + LANES]
    lane = lax.broadcasted_iota(jnp.int32, lora.shape, 1)
    lora = jnp.where(lane < RWKV_N, jnp.tanh(lora), lora)
    wa = _mm(lora, wlo_ref[...])
    w_raw = -_softplus(-(prm[0:1, :] + wa[:, :GROUP_W])) - 0.5
    lw_all = -jnp.exp(w_raw)
    a_all = _sigmoid(prm[1:2, :] + wa[:, GROUP_W:])
    g_all = _mm(_sigmoid(xs[:, 3 * GROUP_W + LANES:3 * GROUP_W + 2 * LANES]), g2_ref[...])
    kk_all = k_all * prm[2:3, :]
    k2_all = k_all * (1.0 + (a_all - 1.0) * prm[3:4, :])
    lc_all = _chunk_cumsum(lw_all)

    lo_mask = lax.broadcasted_iota(jnp.int32, (CHUNK, LANES), 1) < RWKV_N
    n2 = 2 * CHUNK
    ri = lax.broadcasted_iota(jnp.int32, (n2, n2), 0)
    ci = lax.broadcasted_iota(jnp.int32, (n2, n2), 1)
    same = (ri // CHUNK) == (ci // CHUNK)
    strict = same & ((ci % CHUNK) < (ri % CHUNK))
    incl = same & ((ci % CHUNK) <= (ri % CHUNK))

    probs = [(b, p) for b in range(nb) for p in range(N_PAIRS)]

    def sel(arr):
        return [arr[_brows(b), p * LANES:(p + 1) * LANES] for b, p in probs]

    r, v, k2, a_sig, lw, lc = sel(r_all), sel(v_all), sel(k2_all), sel(a_all), sel(lw_all), sel(lc_all)
    kk = [x * lax.rsqrt(_pair_sum(x * x, lo_mask) + 1e-6) for x in sel(kk_all)]
    w_incl = [jnp.exp(x) for x in lc]
    w_inv = [jnp.exp(-x) for x in lc]
    a_hat = [_stack_heads(-x * jnp.exp(c - w), lo_mask) for x, c, w in zip(kk, lc, lw)]
    b_hat = [_stack_heads(x * y * w, lo_mask) for x, y, w in zip(kk, a_sig, w_inv)]
    k_hat = [_stack_heads(x * w, lo_mask) for x, w in zip(k2, w_inv)]
    r_hat = [_stack_heads(x * w, lo_mask) for x, w in zip(r, w_incl)]
    v_st = [_stack_heads(x, lo_mask) for x in v]
    s_old = [s_ref[b, p] for b, p in probs]

    ar = [jnp.concatenate([x, y], axis=0).astype(BF16) for x, y in zip(a_hat, r_hat)]
    bk = [jnp.concatenate([x, y], axis=0).astype(BF16) for x, y in zip(b_hat, k_hat)]
    cross = [_dot(x, y, _NT) for x, y in zip(ar, bk)]
    a_ab = [jnp.where(strict, x[:n2, :n2], 0.0) for x in cross]
    a_ak = [jnp.where(strict, x[:n2, n2:], 0.0) for x in cross]
    m_r = [jnp.concatenate([jnp.where(incl, x[n2:, :n2], 0.0), jnp.where(incl, x[n2:, n2:], 0.0)], axis=1)
           for x in cross]
    t_mat = _inv_unit_lower([-x for x in a_ab])
    ars = [_dot(x, s.astype(BF16), _NT) for x, s in zip(ar, s_old)]
    pv = [_mm(t, x[:n2] + _mm(y, z)) for t, x, y, z in zip(t_mat, ars, a_ak, v_st)]
    pvv = [jnp.concatenate([x, y], axis=0).astype(BF16) for x, y in zip(pv, v_st)]
    y_st = [x[n2:] + _dot(m.astype(BF16), z) for x, m, z in zip(ars, m_r, pvv)]
    s_new = [(s + _dot(x, y, _TN)) * w[CHUNK - 1:CHUNK, :] for s, x, y, w in zip(s_old, pvv, bk, w_incl)]
    for (b, p), s in zip(probs, s_new):
        s_ref[b, p] = s
    for (b, p), ys, rr, kk2, vv in zip(probs, y_st, r, k2, v):
        cols = slice(p * LANES, (p + 1) * LANES)
        y = ys[:CHUNK, :] + ys[CHUNK:, :]
        yc = y - _pair_sum(y, lo_mask) * (1.0 / RWKV_N)
        var = _pair_sum(yc * yc, lo_mask) * (1.0 / RWKV_N)
        y = yc * lax.rsqrt(var + RWKV_LN_EPS) * prm[5:6, cols] + prm[6:7, cols]
        y = y + _pair_sum(rr * kk2 * prm[4:5, cols], lo_mask) * vv
        o_ref[b, :, cols] = y * g_all[_brows(b), cols]


def _rwkv(u, mu, prm, w_lora, g2):
    bsz, seq, _ = u.shape
    return pl.pallas_call(
        _rwkv_kernel,
        out_shape=jax.ShapeDtypeStruct((bsz, seq, GROUP_W), F32),
        grid=(seq // CHUNK,),
        in_specs=[
            pl.BlockSpec((bsz, CHUNK, 4 * GROUP_W), lambda t: (0, t, RWKV_BLK)),
            pl.BlockSpec((1, RWKV_USED), lambda t: (0, 0)),
            pl.BlockSpec((8, GROUP_W), lambda t: (0, 0)),
            pl.BlockSpec((LANES, 2 * GROUP_W), lambda t: (0, 0)),
            pl.BlockSpec((LANES, GROUP_W), lambda t: (0, 0)),
        ],
        out_specs=pl.BlockSpec((bsz, CHUNK, GROUP_W), lambda t: (0, t, 0)),
        scratch_shapes=[
            pltpu.VMEM((bsz, HALO + CHUNK, RWKV_USED), F32),
            pltpu.VMEM((bsz, N_PAIRS, LANES, LANES), F32),
        ],
        compiler_params=pltpu.CompilerParams(
            dimension_semantics=("arbitrary",), vmem_limit_bytes=VMEM_LIMIT),
        name="rwkv",
    )(u, mu, prm, w_lora, g2)


def _outproj_kernel(x_ref, y0_ref, y1_ref, y2_ref, y3_ref, w_ref, o_ref):
    acc = x_ref[...]
    for g, y_ref in enumerate((y0_ref, y1_ref, y2_ref, y3_ref)):
        acc = acc + jnp.dot(y_ref[...].astype(BF16), w_ref[g * GROUP_W:(g + 1) * GROUP_W, :],
                            preferred_element_type=F32)
    o_ref[...] = acc


def _outproj(x2d, ys, w_out, *, tm=512, tn=1024):
    n_tok = x2d.shape[0]
    y_spec = pl.BlockSpec((tm, GROUP_W), lambda i, j: (i, 0))
    return pl.pallas_call(
        _outproj_kernel,
        out_shape=jax.ShapeDtypeStruct((n_tok, D_MODEL), F32),
        grid=(n_tok // tm, D_MODEL // tn),
        in_specs=[pl.BlockSpec((tm, tn), lambda i, j: (i, j)), y_spec, y_spec, y_spec, y_spec,
                  pl.BlockSpec((D_MODEL, tn), lambda i, j: (0, j))],
        out_specs=pl.BlockSpec((tm, tn), lambda i, j: (i, j)),
        compiler_params=pltpu.CompilerParams(
            dimension_semantics=("parallel", "arbitrary"), vmem_limit_bytes=VMEM_LIMIT),
        name="outproj",
    )(x2d, *ys, w_out)


def _ffn_kernel(x_ref, nw_ref, wg_ref, wu_ref, wd_ref, fw_ref, o_ref, n_scr, *, n_f, final):
    f = pl.program_id(1)

    @pl.when(f == 0)
    def _():
        x = x_ref[...]
        n_scr[...] = _rms(x, nw_ref[...]).astype(BF16)
        o_ref[...] = x

    n = n_scr[...]
    gate = jnp.dot(n, wg_ref[...], preferred_element_type=F32)
    up = jnp.dot(n, wu_ref[...], preferred_element_type=F32)
    hid = (_silu(gate) * up).astype(BF16)
    o_ref[...] += jnp.dot(hid, wd_ref[...], preferred_element_type=F32)

    if final:
        @pl.when(f == n_f - 1)
        def _():
            o_ref[...] = _rms(o_ref[...], fw_ref[...])


def _ffn(x2d, norm_w, w_gate, w_up, w_down, final_w, *, final, tm=512, tf=512):
    n_tok = x2d.shape[0]
    n_f = FFN_HIDDEN // tf
    return pl.pallas_call(
        functools.partial(_ffn_kernel, n_f=n_f, final=final),
        out_shape=jax.ShapeDtypeStruct((n_tok, D_MODEL), F32),
        grid=(n_tok // tm, n_f),
        in_specs=[
            pl.BlockSpec((tm, D_MODEL), lambda i, f: (i, 0)),
            pl.BlockSpec((1, D_MODEL), lambda i, f: (0, 0)),
            pl.BlockSpec((D_MODEL, tf), lambda i, f: (0, f)),
            pl.BlockSpec((D_MODEL, tf), lambda i, f: (0, f)),
            pl.BlockSpec((tf, D_MODEL), lambda i, f: (f, 0)),
            pl.BlockSpec((1, D_MODEL), lambda i, f: (0, 0)),
        ],
        out_specs=pl.BlockSpec((tm, D_MODEL), lambda i, f: (i, 0)),
        scratch_shapes=[pltpu.VMEM((tm, D_MODEL), BF16)],
        compiler_params=pltpu.CompilerParams(
            dimension_semantics=("parallel", "arbitrary"), vmem_limit_bytes=VMEM_LIMIT),
        name="ffn_final" if final else "ffn",
    )(x2d, norm_w.reshape(1, D_MODEL), w_gate, w_up, w_down, final_w.reshape(1, D_MODEL))


def _pad_lanes(v, offset, width):
    v = v.astype(F32)
    return jnp.concatenate([jnp.zeros((offset,), F32), v, jnp.zeros((width - offset - v.shape[0],), F32)])


def _rows(rows, width):
    rows = [r.astype(F32).reshape(width) for r in rows]
    return jnp.stack(rows + [jnp.zeros((width,), F32)] * (8 - len(rows)))


def _perm_w_in(w):
    d = w.shape[0]
    gdn_main, gdn_ab = w[:, 0:2048], w[:, 2048:2056]
    ml_main, ml_if = w[:, 2056:4104], w[:, 4104:4112]
    rg = w[:, 4112:5136]
    rw = w[:, 5136:6928]
    gates = jnp.concatenate([gdn_ab, ml_if, jnp.zeros((d, LANES - 16), w.dtype)], axis=1)
    pad = jnp.zeros((d, LANES), w.dtype)
    return jnp.concatenate([gdn_main, ml_main, rw, gates, pad, rg], axis=1).astype(BF16)


def kernel(x, attn_norm, w_in, gdn_conv, gdn_a_log, gdn_dt_bias, gdn_norm, mlstm_conv, mlstm_b_i, mlstm_b_f, mlstm_norm, rglru_conv, rglru_conv_b, rglru_w_a, rglru_b_a, rglru_w_x, rglru_b_x, rglru_lambda, rwkv_mu, rwkv_w0, rwkv_w2, rwkv_a0, rwkv_a2, rwkv_g2, rwkv_k_k, rwkv_k_a, rwkv_r_k, rwkv_ln_w, rwkv_ln_b, w_out, ffn_norm, ffn_w_gate, ffn_w_up, ffn_w_down, final_norm):
    bsz, seq, d = x.shape
    depth = w_in.shape[0]
    x2d = x.reshape(bsz * seq, d)
    for l in range(depth):
        u = _inproj(x2d, attn_norm[l], _perm_w_in(w_in[l])).reshape(bsz, seq, U_COLS)

        gdn_prm = _rows([_pad_lanes(gdn_a_log[l], 0, LANES), _pad_lanes(gdn_dt_bias[l], 0, LANES),
                         gdn_norm[l]], LANES)
        y_gdn = _gdn(u, gdn_conv[l], gdn_prm)

        ml_bias = _pad_lanes(mlstm_b_i[l], 2 * N_HEADS, LANES) + _pad_lanes(mlstm_b_f[l], 3 * N_HEADS, LANES)
        ml_prm = _rows([ml_bias] + [mlstm_norm[l, h] for h in range(N_HEADS)], LANES)
        y_ml = _mlstm(u, mlstm_conv[l], ml_prm)

        rg_prm = _rows([rglru_conv_b[l], rglru_b_a[l], rglru_b_x[l], rglru_lambda[l]], GROUP_W)
        y_rg = _rglru(u, rglru_conv[l], rg_prm, rglru_w_a[l].astype(BF16), rglru_w_x[l].astype(BF16))

        rw_prm = _rows([rwkv_w0[l], rwkv_a0[l], rwkv_k_k[l], rwkv_k_a[l], rwkv_r_k[l],
                        rwkv_ln_w[l], rwkv_ln_b[l]], GROUP_W)
        zeros_lora = jnp.zeros((RWKV_N, GROUP_W), F32)
        w_lora = jnp.concatenate([
            jnp.concatenate([rwkv_w2[l], zeros_lora], axis=1),
            jnp.concatenate([zeros_lora, rwkv_a2[l]], axis=1)], axis=0).astype(BF16)
        y_rw = _rwkv(u, rwkv_mu[l].reshape(1, RWKV_USED), rw_prm, w_lora, rwkv_g2[l].astype(BF16))

        ys = [y.reshape(bsz * seq, GROUP_W) for y in (y_gdn, y_ml, y_rg, y_rw)]
        x2d = _outproj(x2d, ys, w_out[l].astype(BF16))
        x2d = _ffn(x2d, ffn_norm[l], ffn_w_gate[l].astype(BF16), ffn_w_up[l].astype(BF16),
                   ffn_w_down[l].astype(BF16), final_norm, final=(l == depth - 1))
    return x2d.reshape(bsz, seq, d)
```

```python
import functools

import jax
import jax.numpy as jnp
from jax import lax
from jax.experimental import pallas as pl
from jax.experimental.pallas import tpu as pltpu

F32 = jnp.float32
BF16 = jnp.bfloat16

D_MODEL = 2048
GROUP_W = 512
HEAD_D = 128
N_HEADS = 4
RWKV_N = 64
N_PAIRS = GROUP_W // (2 * RWKV_N)
CONV_K = 4
CHUNK = 64
FFN_HIDDEN = 5632
NORM_EPS = 1e-6
RWKV_LN_EPS = 64e-5
RGLRU_C = 8.0
LANES = 128
HALO = 8
INV_LEVELS = 5

U_COLS = 7168
GDN_BLK = 0
MLSTM_BLK = 1
RWKV_BLK = 2
GATES_BLK = 46
RGLRU_BLK = 6
RWKV_USED = 1792

VMEM_LIMIT = 60 * 1024 * 1024

_NN = (((1,), (0,)), ((), ()))
_NT = (((1,), (1,)), ((), ()))
_TN = (((0,), (0,)), ((), ()))


def _dot(a, b, dims=_NN):
    return lax.dot_general(a, b, dims, preferred_element_type=F32)


def _mm(a, b, dims=_NN):
    return _dot(a.astype(BF16), b.astype(BF16), dims)


def _sigmoid(x):
    return 1.0 / (1.0 + jnp.exp(-x))


def _softplus(x):
    return jnp.maximum(x, 0.0) + jnp.log1p(jnp.exp(-jnp.abs(x)))


def _silu(x):
    return x * _sigmoid(x)


def _rms(x, w, eps=NORM_EPS):
    return x * lax.rsqrt(jnp.mean(x * x, axis=-1, keepdims=True) + eps) * w


def _chunk_cumsum(x):
    row = lax.broadcasted_iota(jnp.int32, x.shape, 0) % CHUNK
    d = 1
    while d < CHUNK:
        x = x + jnp.where(row >= d, pltpu.roll(x, d, 0), 0.0)
        d *= 2
    return x


def _inv_unit_lower(a_list):
    n = a_list[0].shape[0]
    eye = (lax.broadcasted_iota(jnp.int32, (n, n), 0) == lax.broadcasted_iota(jnp.int32, (n, n), 1)).astype(F32)
    inv = [eye - a for a in a_list]
    p = [(-a).astype(BF16) for a in a_list]
    for _ in range(INV_LEVELS):
        p = [_dot(x, x).astype(BF16) for x in p]
        inv = [x + _dot(x.astype(BF16), y) for x, y in zip(inv, p)]
    return inv


def _causal_conv(buf, x, conv_w):
    tt = x.shape[0]
    buf[HALO:HALO + tt, :] = x
    acc = None
    for j in range(CONV_K):
        lo = HALO - (CONV_K - 1) + j
        term = conv_w[j:j + 1, :] * buf[lo:lo + tt, :]
        acc = term if acc is None else acc + term
    buf[0:HALO, :] = x[tt - HALO:tt, :]
    return acc


def _brows(b):
    return slice(b * CHUNK, (b + 1) * CHUNK)


def _hcols(h, base=0):
    return slice(base + h * HEAD_D, base + (h + 1) * HEAD_D)


def _inproj_kernel(x_ref, nw_ref, w_ref, o_ref, n_scr):
    @pl.when(pl.program_id(1) == 0)
    def _():
        n_scr[...] = _rms(x_ref[...], nw_ref[...]).astype(BF16)

    o_ref[...] = jnp.dot(n_scr[...], w_ref[...], preferred_element_type=F32)


def _inproj(x2d, norm_w, w_perm, *, tm=1024, tn=1024):
    n_tok = x2d.shape[0]
    return pl.pallas_call(
        _inproj_kernel,
        out_shape=jax.ShapeDtypeStruct((n_tok, U_COLS), F32),
        grid=(n_tok // tm, U_COLS // tn),
        in_specs=[
            pl.BlockSpec((tm, D_MODEL), lambda i, j: (i, 0)),
            pl.BlockSpec((1, D_MODEL), lambda i, j: (0, 0)),
            pl.BlockSpec((D_MODEL, tn), lambda i, j: (0, j)),
        ],
        out_specs=pl.BlockSpec((tm, tn), lambda i, j: (i, j)),
        scratch_shapes=[pltpu.VMEM((tm, D_MODEL), BF16)],
        compiler_params=pltpu.CompilerParams(
            dimension_semantics=("parallel", "arbitrary"), vmem_limit_bytes=VMEM_LIMIT),
        name="inproj",
    )(x2d, norm_w.reshape(1, D_MODEL), w_perm)


def _gdn_kernel(u_ref, gt_ref, cw_ref, prm_ref, o_ref, buf, s_ref):
    nb = u_ref.shape[0]

    @pl.when(pl.program_id(0) == 0)
    def _():
        buf[:, 0:HALO, :] = jnp.zeros((nb, HALO, 3 * GROUP_W), F32)
        s_ref[...] = jnp.zeros_like(s_ref)

    cw = cw_ref[...]
    qkv = _silu(jnp.concatenate(
        [_causal_conv(buf.at[b], u_ref[b, :, 0:3 * GROUP_W], cw) for b in range(nb)], axis=0))
    gates = gt_ref[...].reshape(nb * CHUNK, LANES)
    prm = prm_ref[...]
    g_all = -jnp.exp(prm[0:1, :]) * _softplus(gates + prm[1:2, :])
    beta_all = _sigmoid(gates)
    gc_all = _chunk_cumsum(g_all)
    norm_w = prm[2:3, :]
    gc_t = [gc_all[_brows(b), :].T for b in range(nb)]

    qn, kn = [], []
    for h in range(N_HEADS):
        q = qkv[:, _hcols(h)]
        k = qkv[:, _hcols(h, GROUP_W)]
        qn.append(q * (lax.rsqrt(jnp.sum(q * q, axis=-1, keepdims=True) + 1e-6) * (HEAD_D ** -0.5)))
        kn.append(k * lax.rsqrt(jnp.sum(k * k, axis=-1, keepdims=True) + 1e-6))

    ri = lax.broadcasted_iota(jnp.int32, (CHUNK, CHUNK), 0)
    ci = lax.broadcasted_iota(jnp.int32, (CHUNK, CHUNK), 1)
    causal = ri >= ci
    strict = ri > ci

    probs = [(b, h) for b in range(nb) for h in range(N_HEADS)]
    q = [qn[h][_brows(b), :] for b, h in probs]
    k = [kn[h][_brows(b), :] for b, h in probs]
    v = [qkv[_brows(b), _hcols(h, 2 * GROUP_W)] for b, h in probs]
    beta = [beta_all[_brows(b), N_HEADS + h:N_HEADS + h + 1] for b, h in probs]
    gcol = [gc_all[_brows(b), h:h + 1] for b, h in probs]
    grow = [gc_t[b][h:h + 1, :] for b, h in probs]
    decay = [jnp.where(causal, jnp.exp(jnp.where(causal, gc - gr, 0.0)), 0.0) for gc, gr in zip(gcol, grow)]
    kb = [x * y for x, y in zip(k, beta)]
    raw = [_mm(jnp.concatenate([x, y], axis=0), z, _NT) for x, y, z in zip(kb, q, k)]
    a_mat = [jnp.where(strict, r[:CHUNK] * d, 0.0) for r, d in zip(raw, decay)]
    attn = [r[CHUNK:] * d for r, d in zip(raw, decay)]
    t_mat = _inv_unit_lower(a_mat)
    eg = [jnp.exp(x) for x in gcol]
    sol = [_mm(t, jnp.concatenate([x * y, z * e], axis=1))
           for t, x, y, z, e in zip(t_mat, v, beta, kb, eg)]
    s_old = [s_ref[b, h] for b, h in probs]
    ws = [_mm(jnp.concatenate([x[:, HEAD_D:], y * e], axis=0), s)
          for x, y, e, s in zip(sol, q, eg, s_old)]
    v_new = [x[:, :HEAD_D] - y[:CHUNK] for x, y in zip(sol, ws)]
    o = [y[CHUNK:] + _mm(a, x) for y, a, x in zip(ws, attn, v_new)]
    glast = [x[CHUNK - 1:CHUNK, :] for x in gcol]
    s_new = [s * jnp.exp(gl) + _mm(x * jnp.exp(gl - gc), vn, _TN)
             for s, gl, x, gc, vn in zip(s_old, glast, k, gcol, v_new)]
    for (b, h), s in zip(probs, s_new):
        s_ref[b, h] = s
    for (b, h), x in zip(probs, o):
        z = u_ref[b, :, _hcols(h, 3 * GROUP_W)]
        o_ref[b, :, _hcols(h)] = (_rms(x, norm_w) * _silu(z)).astype(BF16)


def _gdn(u, conv_w, prm):
    bsz, seq, _ = u.shape
    return pl.pallas_call(
        _gdn_kernel,
        out_shape=jax.ShapeDtypeStruct((bsz, seq, GROUP_W), BF16),
        grid=(seq // CHUNK,),
        in_specs=[
            pl.BlockSpec((bsz, CHUNK, 4 * GROUP_W), lambda t: (0, t, GDN_BLK)),
            pl.BlockSpec((bsz, CHUNK, LANES), lambda t: (0, t, GATES_BLK)),
            pl.BlockSpec((CONV_K, 3 * GROUP_W), lambda t: (0, 0)),
            pl.BlockSpec((8, LANES), lambda t: (0, 0)),
        ],
        out_specs=pl.BlockSpec((bsz, CHUNK, GROUP_W), lambda t: (0, t, 0)),
        scratch_shapes=[
            pltpu.VMEM((bsz, HALO + CHUNK, 3 * GROUP_W), F32),
            pltpu.VMEM((bsz, N_HEADS, HEAD_D, HEAD_D), F32),
        ],
        compiler_params=pltpu.CompilerParams(
            dimension_semantics=("arbitrary",), vmem_limit_bytes=VMEM_LIMIT),
        name="gdn",
    )(u, u, conv_w, prm)


def _mlstm_kernel(u_ref, gt_ref, cw_ref, prm_ref, o_ref, buf, c_ref, n_ref, m_ref):
    nb = u_ref.shape[0]

    @pl.when(pl.program_id(0) == 0)
    def _():
        buf[:, 0:HALO, :] = jnp.zeros((nb, HALO, 2 * GROUP_W), F32)
        c_ref[...] = jnp.zeros_like(c_ref)
        n_ref[...] = jnp.zeros_like(n_ref)
        m_ref[...] = jnp.zeros_like(m_ref)

    cw = cw_ref[...]
    qk = _silu(jnp.concatenate(
        [_causal_conv(buf.at[b], u_ref[b, :, 0:2 * GROUP_W], cw) for b in range(nb)], axis=0))
    gates = gt_ref[...].reshape(nb * CHUNK, LANES)
    prm = prm_ref[...]
    pre = gates + prm[0:1, :]
    b_all = _chunk_cumsum(-_softplus(-pre))
    b_tr = [b_all[_brows(b), :].T for b in range(nb)]
    i_tr = [pre[_brows(b), :].T for b in range(nb)]

    ri = lax.broadcasted_iota(jnp.int32, (CHUNK, CHUNK), 0)
    ci = lax.broadcasted_iota(jnp.int32, (CHUNK, CHUNK), 1)
    causal = ri >= ci

    probs = [(b, h) for b in range(nb) for h in range(N_HEADS)]
    li = [2 * N_HEADS + h for _, h in probs]
    lf = [3 * N_HEADS + h for _, h in probs]
    q = [qk[_brows(b), _hcols(h)] for b, h in probs]
    k = [qk[_brows(b), _hcols(h, GROUP_W)] * (HEAD_D ** -0.5) for b, h in probs]
    v = [u_ref[b, :, _hcols(h, 2 * GROUP_W)] for b, h in probs]
    bcol = [b_all[_brows(b), l:l + 1] for (b, _), l in zip(probs, lf)]
    icol = [pre[_brows(b), l:l + 1] for (b, _), l in zip(probs, li)]
    brow = [b_tr[b][l:l + 1, :] for (b, _), l in zip(probs, lf)]
    irow = [i_tr[b][l:l + 1, :] for (b, _), l in zip(probs, li)]
    m_old = [m_ref[b, h][0:1, 0:1] for b, h in probs]
    c_old = [c_ref[b, h] for b, h in probs]
    n_old = [n_ref[b, h][0:1, :] for b, h in probs]

    d = [jnp.where(causal, bc - br + ir, -jnp.inf) for bc, br, ir in zip(bcol, brow, irow)]
    inter = [bc + m for bc, m in zip(bcol, m_old)]
    m_t = [jnp.maximum(x, jnp.max(y, axis=-1, keepdims=True)) for x, y in zip(inter, d)]
    qk_raw = [_mm(x, y, _NT) for x, y in zip(q, k)]
    s = [r * jnp.exp(x - m) for r, x, m in zip(qk_raw, d, m_t)]
    a = [jnp.exp(x - m) for x, m in zip(inter, m_t)]
    qc = [_mm(x, c) for x, c in zip(q, c_old)]
    sv = [_mm(x, y) for x, y in zip(s, v)]
    num = [x * y + z for x, y, z in zip(a, qc, sv)]
    den = [x * jnp.sum(y * n, axis=-1, keepdims=True) + jnp.sum(z, axis=-1, keepdims=True)
           for x, y, n, z in zip(a, q, n_old, s)]
    h_t = [x / jnp.maximum(jnp.abs(y), jnp.exp(-m)) for x, y, m in zip(num, den, m_t)]
    g = [x[CHUNK - 1:CHUNK, :] for x in bcol]
    w_log = [x - bc + ic for x, bc, ic in zip(g, bcol, icol)]
    m_new = [jnp.maximum(x + m, jnp.max(w, axis=0, keepdims=True)) for x, m, w in zip(g, m_old, w_log)]
    scale = [jnp.exp(x + m - mn) for x, m, mn in zip(g, m_old, m_new)]
    kw = [x * jnp.exp(w - mn) for x, w, mn in zip(k, w_log, m_new)]
    c_new = [sc * c + _mm(x, y, _TN) for sc, c, x, y in zip(scale, c_old, kw, v)]
    n_new = [sc * n + jnp.sum(x, axis=0, keepdims=True) for sc, n, x in zip(scale, n_old, kw)]
    for (b, h), c, n, m in zip(probs, c_new, n_new, m_new):
        c_ref[b, h] = c
        n_ref[b, h] = jnp.broadcast_to(n, (8, HEAD_D))
        m_ref[b, h] = jnp.broadcast_to(m, (8, HEAD_D))
    for (b, h), x in zip(probs, h_t):
        o_gate = u_ref[b, :, _hcols(h, 3 * GROUP_W)]
        o_ref[b, :, _hcols(h)] = _rms(_sigmoid(o_gate) * x, prm[1 + h:2 + h, :]).astype(BF16)


def _mlstm(u, conv_w, prm):
    bsz, seq, _ = u.shape
    return pl.pallas_call(
        _mlstm_kernel,
        out_shape=jax.ShapeDtypeStruct((bsz, seq, GROUP_W), BF16),
        grid=(seq // CHUNK,),
        in_specs=[
            pl.BlockSpec((bsz, CHUNK, 4 * GROUP_W), lambda t: (0, t, MLSTM_BLK)),
            pl.BlockSpec((bsz, CHUNK, LANES), lambda t: (0, t, GATES_BLK)),
            pl.BlockSpec((CONV_K, 2 * GROUP_W), lambda t: (0, 0)),
            pl.BlockSpec((8, LANES), lambda t: (0, 0)),
        ],
        out_specs=pl.BlockSpec((bsz, CHUNK, GROUP_W), lambda t: (0, t, 0)),
        scratch_shapes=[
            pltpu.VMEM((bsz, HALO + CHUNK, 2 * GROUP_W), F32),
            pltpu.VMEM((bsz, N_HEADS, HEAD_D, HEAD_D), F32),
            pltpu.VMEM((bsz, N_HEADS, 8, HEAD_D), F32),
            pltpu.VMEM((bsz, N_HEADS, 8, HEAD_D), F32),
        ],
        compiler_params=pltpu.CompilerParams(
            dimension_semantics=("arbitrary",), vmem_limit_bytes=VMEM_LIMIT),
        name="mlstm",
    )(u, u, conv_w, prm)


def _rglru_kernel(u_ref, cw_ref, prm_ref, wa_ref, wx_ref, o_ref, buf, h_ref):
    @pl.when(pl.program_id(1) == 0)
    def _():
        buf[0:HALO, :] = jnp.zeros((HALO, GROUP_W), F32)
        h_ref[...] = jnp.zeros_like(h_ref)

    tt = u_ref.shape[1]
    prm = prm_ref[...]
    xb = _causal_conv(buf, u_ref[0, :, 0:GROUP_W], cw_ref[...]) + prm[0:1, :]
    ra = []
    rx = []
    for n in range(N_HEADS):
        blk = xb[:, _hcols(n)].astype(BF16)
        ra.append(_dot(blk, wa_ref[n]))
        rx.append(_dot(blk, wx_ref[n]))
    r = _sigmoid(jnp.concatenate(ra, axis=1) + prm[1:2, :])
    i = _sigmoid(jnp.concatenate(rx, axis=1) + prm[2:3, :])
    log_a = -RGLRU_C * r * _softplus(-prm[3:4, :])
    a = jnp.exp(log_a)
    th = jnp.tanh(log_a)
    b = jnp.sqrt(-2.0 * th / (1.0 - th)) * (i * xb)
    row = lax.broadcasted_iota(jnp.int32, (tt, GROUP_W), 0)
    d = 1
    while d < tt:
        keep = row >= d
        b = jnp.where(keep, a * pltpu.roll(b, d, 0) + b, b)
        a = jnp.where(keep, a * pltpu.roll(a, d, 0), a)
        d *= 2
    h = b + a * h_ref[0:1, :]
    h_ref[...] = jnp.broadcast_to(h[tt - 1:tt, :], (8, GROUP_W))
    gate = u_ref[0, :, GROUP_W:2 * GROUP_W]
    gelu = 0.5 * gate * (1.0 + jnp.tanh(0.7978845608028654 * (gate + 0.044715 * gate * gate * gate)))
    o_ref[0] = (h * gelu).astype(BF16)


def _rglru(u, conv_w, prm, w_a, w_x, *, tt=256):
    bsz, seq, _ = u.shape
    return pl.pallas_call(
        _rglru_kernel,
        out_shape=jax.ShapeDtypeStruct((bsz, seq, GROUP_W), BF16),
        grid=(bsz, seq // tt),
        in_specs=[
            pl.BlockSpec((1, tt, 2 * GROUP_W), lambda b, t: (b, t, RGLRU_BLK)),
            pl.BlockSpec((CONV_K, GROUP_W), lambda b, t: (0, 0)),
            pl.BlockSpec((8, GROUP_W), lambda b, t: (0, 0)),
            pl.BlockSpec((N_HEADS, HEAD_D, HEAD_D), lambda b, t: (0, 0, 0)),
            pl.BlockSpec((N_HEADS, HEAD_D, HEAD_D), lambda b, t: (0, 0, 0)),
        ],
        out_specs=pl.BlockSpec((1, tt, GROUP_W), lambda b, t: (b, t, 0)),
        scratch_shapes=[
            pltpu.VMEM((HALO + tt, GROUP_W), F32),
            pltpu.VMEM((8, GROUP_W), F32),
        ],
        compiler_params=pltpu.CompilerParams(
            dimension_semantics=("parallel", "arbitrary"), vmem_limit_bytes=VMEM_LIMIT),
        name="rglru",
    )(u, conv_w, prm, w_a, w_x)


def _pair_sum(x, lo_mask):
    s_lo = jnp.sum(jnp.where(lo_mask, x, 0.0), axis=-1, keepdims=True)
    s_hi = jnp.sum(jnp.where(lo_mask, 0.0, x), axis=-1, keepdims=True)
    return jnp.where(lo_mask, s_lo, s_hi)


def _stack_heads(x, lo_mask):
    return jnp.concatenate([jnp.where(lo_mask, x, 0.0), jnp.where(lo_mask, 0.0, x)], axis=0)


def _rwkv_kernel(u_ref, mu_ref, prm_ref, wlo_ref, g2_ref, o_ref, buf, s_ref):
    nb = u_ref.shape[0]

    @pl.when(pl.program_id(0) == 0)
    def _():
        buf[:, 0:HALO, :] = jnp.zeros((nb, HALO, RWKV_USED), F32)
        s_ref[...] = jnp.zeros_like(s_ref)

    mu = mu_ref[...]
    xs = []
    for b in range(nb):
        x = u_ref[b, :, 0:RWKV_USED]
        buf[b, HALO:HALO + CHUNK, :] = x
        x_prev = buf[b, HALO - 1:HALO - 1 + CHUNK, :]
        buf[b, 0:HALO, :] = x[CHUNK - HALO:CHUNK, :]
        xs.append(x + (x_prev - x) * mu)
    xs = jnp.concatenate(xs, axis=0)
    prm = prm_ref[...]
    r_all = xs[:, 0:GROUP_W]
    k_all = xs[:, GROUP_W:2 * GROUP_W]
    v_all = xs[:, 2 * GROUP_W:3 * GROUP_W]
    lora = xs[:, 3 * GROUP_W:3 * GROUP_W + LANES]
    lane = lax.broadcasted_iota(jnp.int32, lora.shape, 1)
    lora = jnp.where(lane < RWKV_N, jnp.tanh(lora), lora)
    wa = _mm(lora, wlo_ref[...])
    w_raw = -_softplus(-(prm[0:1, :] + wa[:, :GROUP_W])) - 0.5
    lw_all = -jnp.exp(w_raw)
    a_all = _sigmoid(prm[1:2, :] + wa[:, GROUP_W:])
    g_all = _mm(_sigmoid(xs[:, 3 * GROUP_W + LANES:3 * GROUP_W + 2 * LANES]), g2_ref[...])
    kk_all = k_all * prm[2:3, :]
    k2_all = k_all * (1.0 + (a_all - 1.0) * prm[3:4, :])
    lc_all = _chunk_cumsum(lw_all)

    lo_mask = lax.broadcasted_iota(jnp.int32, (CHUNK, LANES), 1) < RWKV_N
    n2 = 2 * CHUNK
    ri = lax.broadcasted_iota(jnp.int32, (n2, n2), 0)
    ci = lax.broadcasted_iota(jnp.int32, (n2, n2), 1)
    same = (ri // CHUNK) == (ci // CHUNK)
    strict = same & ((ci % CHUNK) < (ri % CHUNK))
    incl = same & ((ci % CHUNK) <= (ri % CHUNK))

    probs = [(b, p) for b in range(nb) for p in range(N_PAIRS)]

    def sel(arr):
        return [arr[_brows(b), p * LANES:(p + 1) * LANES] for b, p in probs]

    r, v, k2, a_sig, lw, lc = sel(r_all), sel(v_all), sel(k2_all), sel(a_all), sel(lw_all), sel(lc_all)
    kk = [x * lax.rsqrt(_pair_sum(x * x, lo_mask) + 1e-6) for x in sel(kk_all)]
    w_incl = [jnp.exp(x) for x in lc]
    w_inv = [jnp.exp(-x) for x in lc]
    a_hat = [_stack_heads(-x * jnp.exp(c - w), lo_mask) for x, c, w in zip(kk, lc, lw)]
    b_hat = [_stack_heads(x * y * w, lo_mask) for x, y, w in zip(kk, a_sig, w_inv)]
    k_hat = [_stack_heads(x * w, lo_mask) for x, w in zip(k2, w_inv)]
    r_hat = [_stack_heads(x * w, lo_mask) for x, w in zip(r, w_incl)]
    v_st = [_stack_heads(x, lo_mask) for x in v]
    s_old = [s_ref[b, p] for b, p in probs]

    ar = [jnp.concatenate([x, y], axis=0).astype(BF16) for x, y in zip(a_hat, r_hat)]
    bk = [jnp.concatenate([x, y], axis=0).astype(BF16) for x, y in zip(b_hat, k_hat)]
    cross = [_dot(x, y, _NT) for x, y in zip(ar, bk)]
    a_ab = [jnp.where(strict, x[:n2, :n2], 0.0) for x in cross]
    a_ak = [jnp.where(strict, x[:n2, n2:], 0.0) for x in cross]
    m_r = [jnp.concatenate([jnp.where(incl, x[n2:, :n2], 0.0), jnp.where(incl, x[n2:, n2:], 0.0)], axis=1)
           for x in cross]
    t_mat = _inv_unit_lower([-x for x in a_ab])
    ars = [_dot(x, s.astype(BF16), _NT) for x, s in zip(ar, s_old)]
    pv = [_mm(t, x[:n2] + _mm(y, z)) for t, x, y, z in zip(t_mat, ars, a_ak, v_st)]
    pvv = [jnp.concatenate([x, y], axis=0).astype(BF16) for x, y in zip(pv, v_st)]
    y_st = [x[n2:] + _dot(m.astype(BF16), z) for x, m, z in zip(ars, m_r, pvv)]
    s_new = [(s + _dot(x, y, _TN)) * w[CHUNK - 1:CHUNK, :] for s, x, y, w in zip(s_old, pvv, bk, w_incl)]
    for (b, p), s in zip(probs, s_new):
        s_ref[b, p] = s
    for (b, p), ys, rr, kk2, vv in zip(probs, y_st, r, k2, v):
        cols = slice(p * LANES, (p + 1) * LANES)
        y = ys[:CHUNK, :] + ys[CHUNK:, :]
        yc = y - _pair_sum(y, lo_mask) * (1.0 / RWKV_N)
        var = _pair_sum(yc * yc, lo_mask) * (1.0 / RWKV_N)
        y = yc * lax.rsqrt(var + RWKV_LN_EPS) * prm[5:6, cols] + prm[6:7, cols]
        y = y + _pair_sum(rr * kk2 * prm[4:5, cols], lo_mask) * vv
        o_ref[b, :, cols] = (y * g_all[_brows(b), cols]).astype(BF16)


def _rwkv(u, mu, prm, w_lora, g2):
    bsz, seq, _ = u.shape
    return pl.pallas_call(
        _rwkv_kernel,
        out_shape=jax.ShapeDtypeStruct((bsz, seq, GROUP_W), BF16),
        grid=(seq // CHUNK,),
        in_specs=[
            pl.BlockSpec((bsz, CHUNK, 4 * GROUP_W), lambda t: (0, t, RWKV_BLK)),
            pl.BlockSpec((1, RWKV_USED), lambda t: (0, 0)),
            pl.BlockSpec((8, GROUP_W), lambda t: (0, 0)),
            pl.BlockSpec((LANES, 2 * GROUP_W), lambda t: (0, 0)),
            pl.BlockSpec((LANES, GROUP_W), lambda t: (0, 0)),
        ],
        out_specs=pl.BlockSpec((bsz, CHUNK, GROUP_W), lambda t: (0, t, 0)),
        scratch_shapes=[
            pltpu.VMEM((bsz, HALO + CHUNK, RWKV_USED), F32),
            pltpu.VMEM((bsz, N_PAIRS, LANES, LANES), F32),
        ],
        compiler_params=pltpu.CompilerParams(
            dimension_semantics=("arbitrary",), vmem_limit_bytes=VMEM_LIMIT),
        name="rwkv",
    )(u, mu, prm, w_lora, g2)


def _outproj_kernel(x_ref, y0_ref, y1_ref, y2_ref, y3_ref, w_ref, o_ref):
    acc = x_ref[...]
    for g, y_ref in enumerate((y0_ref, y1_ref, y2_ref, y3_ref)):
        acc = acc + _dot(y_ref[...], w_ref[g * GROUP_W:(g + 1) * GROUP_W, :])
    o_ref[...] = acc


def _outproj(x2d, ys, w_out, *, tm=512):
    n_tok = x2d.shape[0]
    y_spec = pl.BlockSpec((tm, GROUP_W), lambda i: (i, 0))
    return pl.pallas_call(
        _outproj_kernel,
        out_shape=jax.ShapeDtypeStruct((n_tok, D_MODEL), F32),
        grid=(n_tok // tm,),
        in_specs=[pl.BlockSpec((tm, D_MODEL), lambda i: (i, 0)), y_spec, y_spec, y_spec, y_spec,
                  pl.BlockSpec((D_MODEL, D_MODEL), lambda i: (0, 0))],
        out_specs=pl.BlockSpec((tm, D_MODEL), lambda i: (i, 0)),
        compiler_params=pltpu.CompilerParams(
            dimension_semantics=("parallel",), vmem_limit_bytes=VMEM_LIMIT),
        name="outproj",
    )(x2d, *ys, w_out)


def _ffn_kernel(x_ref, nw_ref, wg_ref, wu_ref, wd_ref, fw_ref, o_ref, n_scr, *, n_f, final):
    f = pl.program_id(1)

    @pl.when(f == 0)
    def _():
        x = x_ref[...]
        n_scr[...] = _rms(x, nw_ref[...]).astype(BF16)
        o_ref[...] = x

    n = n_scr[...]
    gate = jnp.dot(n, wg_ref[...], preferred_element_type=F32)
    up = jnp.dot(n, wu_ref[...], preferred_element_type=F32)
    hid = (_silu(gate) * up).astype(BF16)
    o_ref[...] += jnp.dot(hid, wd_ref[...], preferred_element_type=F32)

    if final:
        @pl.when(f == n_f - 1)
        def _():
            o_ref[...] = _rms(o_ref[...], fw_ref[...])


def _ffn(x2d, norm_w, w_gate, w_up, w_down, final_w, *, final, tm=1024, tf=512):
    n_tok = x2d.shape[0]
    n_f = FFN_HIDDEN // tf
    return pl.pallas_call(
        functools.partial(_ffn_kernel, n_f=n_f, final=final),
        out_shape=jax.ShapeDtypeStruct((n_tok, D_MODEL), F32),
        grid=(n_tok // tm, n_f),
        in_specs=[
            pl.BlockSpec((tm, D_MODEL), lambda i, f: (i, 0)),
            pl.BlockSpec((1, D_MODEL), lambda i, f: (0, 0)),
            pl.BlockSpec((D_MODEL, tf), lambda i, f: (0, f)),
            pl.BlockSpec((D_MODEL, tf), lambda i, f: (0, f)),
            pl.BlockSpec((tf, D_MODEL), lambda i, f: (f, 0)),
            pl.BlockSpec((1, D_MODEL), lambda i, f: (0, 0)),
        ],
        out_specs=pl.BlockSpec((tm, D_MODEL), lambda i, f: (i, 0)),
        scratch_shapes=[pltpu.VMEM((tm, D_MODEL), BF16)],
        compiler_params=pltpu.CompilerParams(
            dimension_semantics=("parallel", "arbitrary"), vmem_limit_bytes=VMEM_LIMIT),
        name="ffn_final" if final else "ffn",
    )(x2d, norm_w.reshape(1, D_MODEL), w_gate, w_up, w_down, final_w.reshape(1, D_MODEL))


def _pad_lanes(v, offset, width):
    v = v.astype(F32)
    return jnp.concatenate([jnp.zeros((offset,), F32), v, jnp.zeros((width - offset - v.shape[0],), F32)])


def _rows(rows, width):
    rows = [r.astype(F32).reshape(width) for r in rows]
    return jnp.stack(rows + [jnp.zeros((width,), F32)] * (8 - len(rows)))


def _perm_w_in(w):
    w = w.astype(BF16)
    d = w.shape[0]
    gdn_main, gdn_ab = w[:, 0:2048], w[:, 2048:2056]
    ml_main, ml_if = w[:, 2056:4104], w[:, 4104:4112]
    rg = w[:, 4112:5136]
    rw = w[:, 5136:6928]
    pad = jnp.zeros((d, 2 * LANES - 16), BF16)
    return jnp.concatenate([gdn_main, ml_main, rw, gdn_ab, ml_if, pad, rg], axis=1)


def kernel(x, attn_norm, w_in, gdn_conv, gdn_a_log, gdn_dt_bias, gdn_norm, mlstm_conv, mlstm_b_i, mlstm_b_f, mlstm_norm, rglru_conv, rglru_conv_b, rglru_w_a, rglru_b_a, rglru_w_x, rglru_b_x, rglru_lambda, rwkv_mu, rwkv_w0, rwkv_w2, rwkv_a0, rwkv_a2, rwkv_g2, rwkv_k_k, rwkv_k_a, rwkv_r_k, rwkv_ln_w, rwkv_ln_b, w_out, ffn_norm, ffn_w_gate, ffn_w_up, ffn_w_down, final_norm):
    bsz, seq, d = x.shape
    depth = w_in.shape[0]
    x2d = x.reshape(bsz * seq, d)
    for l in range(depth):
        u = _inproj(x2d, attn_norm[l], _perm_w_in(w_in[l])).reshape(bsz, seq, U_COLS)

        gdn_prm = _rows([_pad_lanes(gdn_a_log[l], 0, LANES), _pad_lanes(gdn_dt_bias[l], 0, LANES),
                         gdn_norm[l]], LANES)
        y_gdn = _gdn(u, gdn_conv[l], gdn_prm)

        ml_bias = _pad_lanes(mlstm_b_i[l], 2 * N_HEADS, LANES) + _pad_lanes(mlstm_b_f[l], 3 * N_HEADS, LANES)
        ml_prm = _rows([ml_bias] + [mlstm_norm[l, h] for h in range(N_HEADS)], LANES)
        y_ml = _mlstm(u, mlstm_conv[l], ml_prm)

        rg_prm = _rows([rglru_conv_b[l], rglru_b_a[l], rglru_b_x[l], rglru_lambda[l]], GROUP_W)
        y_rg = _rglru(u, rglru_conv[l], rg_prm, rglru_w_a[l].astype(BF16), rglru_w_x[l].astype(BF16))

        rw_prm = _rows([rwkv_w0[l], rwkv_a0[l], rwkv_k_k[l], rwkv_k_a[l], rwkv_r_k[l],
                        rwkv_ln_w[l], rwkv_ln_b[l]], GROUP_W)
        zeros_lora = jnp.zeros((RWKV_N, GROUP_W), F32)
        w_lora = jnp.concatenate([
            jnp.concatenate([rwkv_w2[l], zeros_lora], axis=1),
            jnp.concatenate([zeros_lora, rwkv_a2[l]], axis=1)], axis=0).astype(BF16)
        y_rw = _rwkv(u, rwkv_mu[l].reshape(1, RWKV_USED), rw_prm, w_lora, rwkv_g2[l].astype(BF16))

        ys = [y.reshape(bsz * seq, GROUP_W) for y in (y_gdn, y_ml, y_rg, y_rw)]
        x2d = _outproj(x2d, ys, w_out[l].astype(BF16))
        x2d = _ffn(x2d, ffn_norm[l], ffn_w_gate[l].astype(BF16), ffn_w_up[l].astype(BF16),
                   ffn_w_down[l].astype(BF16), final_norm, final=(l == depth - 1))
    return x2d.reshape(bsz, seq, d)
```

```python
import functools

import jax
import jax.numpy as jnp
from jax import lax
from jax.experimental import pallas as pl
from jax.experimental.pallas import tpu as pltpu

F32 = jnp.float32
BF16 = jnp.bfloat16

D_MODEL = 2048
GROUP_W = 512
HEAD_D = 128
N_HEADS = 4
RWKV_N = 64
N_PAIRS = GROUP_W // (2 * RWKV_N)
CONV_K = 4
CHUNK = 64
FFN_HIDDEN = 5632
NORM_EPS = 1e-6
RWKV_LN_EPS = 64e-5
RGLRU_C = 8.0
LANES = 128
HALO = 8
INV_LEVELS = 5

U_COLS = 7168
GDN_BLK = 0
MLSTM_BLK = 1
RWKV_BLK = 2
GATES_BLK = 46
RGLRU_BLK = 6
RWKV_USED = 1792

VMEM_LIMIT = 60 * 1024 * 1024

_NN = (((1,), (0,)), ((), ()))
_NT = (((1,), (1,)), ((), ()))
_TN = (((0,), (0,)), ((), ()))


def _dot(a, b, dims=_NN):
    return lax.dot_general(a, b, dims, preferred_element_type=F32)


def _mm(a, b, dims=_NN):
    return _dot(a.astype(BF16), b.astype(BF16), dims)


def _sigmoid(x):
    return 1.0 / (1.0 + jnp.exp(-x))


def _softplus(x):
    return jnp.maximum(x, 0.0) + jnp.log1p(jnp.exp(-jnp.abs(x)))


def _silu(x):
    return x * _sigmoid(x)


def _rms(x, w, eps=NORM_EPS):
    return x * lax.rsqrt(jnp.mean(x * x, axis=-1, keepdims=True) + eps) * w


def _chunk_cumsum(x):
    row = lax.broadcasted_iota(jnp.int32, x.shape, 0) % CHUNK
    d = 1
    while d < CHUNK:
        x = x + jnp.where(row >= d, pltpu.roll(x, d, 0), 0.0)
        d *= 2
    return x


def _inv_unit_lower(a_list):
    n = a_list[0].shape[0]
    eye = (lax.broadcasted_iota(jnp.int32, (n, n), 0) == lax.broadcasted_iota(jnp.int32, (n, n), 1)).astype(F32)
    inv = [eye - a for a in a_list]
    p = [(-a).astype(BF16) for a in a_list]
    for _ in range(INV_LEVELS):
        p = [_dot(x, x).astype(BF16) for x in p]
        inv = [x + _dot(x.astype(BF16), y) for x, y in zip(inv, p)]
    return inv


def _causal_conv(buf, x, conv_w):
    tt = x.shape[0]
    buf[HALO:HALO + tt, :] = x
    xx = buf[...]
    acc = conv_w[0:1, :] * xx
    for j in range(1, CONV_K):
        acc = conv_w[j:j + 1, :] * xx + pltpu.roll(acc, 1, 0)
    buf[0:HALO, :] = x[tt - HALO:tt, :]
    return acc[HALO:, :]


def _brows(b):
    return slice(b * CHUNK, (b + 1) * CHUNK)


def _hcols(h, base=0):
    return slice(base + h * HEAD_D, base + (h + 1) * HEAD_D)


def _inproj_kernel(x_ref, nw_ref, w_ref, o_ref, n_scr):
    @pl.when(pl.program_id(1) == 0)
    def _():
        n_scr[...] = _rms(x_ref[...], nw_ref[...]).astype(BF16)

    o_ref[...] = jnp.dot(n_scr[...], w_ref[...], preferred_element_type=F32)


def _inproj(x2d, norm_w, w_perm, *, tm=1024, tn=1024):
    n_tok = x2d.shape[0]
    return pl.pallas_call(
        _inproj_kernel,
        out_shape=jax.ShapeDtypeStruct((n_tok, U_COLS), F32),
        grid=(n_tok // tm, U_COLS // tn),
        in_specs=[
            pl.BlockSpec((tm, D_MODEL), lambda i, j: (i, 0)),
            pl.BlockSpec((1, D_MODEL), lambda i, j: (0, 0)),
            pl.BlockSpec((D_MODEL, tn), lambda i, j: (0, j)),
        ],
        out_specs=pl.BlockSpec((tm, tn), lambda i, j: (i, j)),
        scratch_shapes=[pltpu.VMEM((tm, D_MODEL), BF16)],
        compiler_params=pltpu.CompilerParams(
            dimension_semantics=("parallel", "arbitrary"), vmem_limit_bytes=VMEM_LIMIT),
        name="inproj",
    )(x2d, norm_w.reshape(1, D_MODEL), w_perm)


def _gdn_kernel(u_ref, gt_ref, cw_ref, prm_ref, o_ref, buf, s_ref):
    nb = u_ref.shape[0]

    @pl.when(pl.program_id(0) == 0)
    def _():
        buf[:, 0:HALO, :] = jnp.zeros((nb, HALO, 3 * GROUP_W), F32)
        s_ref[...] = jnp.zeros_like(s_ref)

    cw = cw_ref[...]
    qkv = _silu(jnp.concatenate(
        [_causal_conv(buf.at[b], u_ref[b, :, 0:3 * GROUP_W], cw) for b in range(nb)], axis=0))
    gates = gt_ref[...].reshape(nb * CHUNK, LANES)
    prm = prm_ref[...]
    g_all = -jnp.exp(prm[0:1, :]) * _softplus(gates + prm[1:2, :])
    beta_all = _sigmoid(gates)
    gc_all = _chunk_cumsum(g_all)
    norm_w = prm[2:3, :]
    gc_t = [gc_all[_brows(b), :].T for b in range(nb)]

    qn, kn = [], []
    for h in range(N_HEADS):
        q = qkv[:, _hcols(h)]
        k = qkv[:, _hcols(h, GROUP_W)]
        qn.append(q * (lax.rsqrt(jnp.sum(q * q, axis=-1, keepdims=True) + 1e-6) * (HEAD_D ** -0.5)))
        kn.append(k * lax.rsqrt(jnp.sum(k * k, axis=-1, keepdims=True) + 1e-6))

    ri = lax.broadcasted_iota(jnp.int32, (CHUNK, CHUNK), 0)
    ci = lax.broadcasted_iota(jnp.int32, (CHUNK, CHUNK), 1)
    causal = ri >= ci
    strict = ri > ci

    probs = [(b, h) for b in range(nb) for h in range(N_HEADS)]
    q = [qn[h][_brows(b), :] for b, h in probs]
    k = [kn[h][_brows(b), :] for b, h in probs]
    v = [qkv[_brows(b), _hcols(h, 2 * GROUP_W)] for b, h in probs]
    beta = [beta_all[_brows(b), N_HEADS + h:N_HEADS + h + 1] for b, h in probs]
    gcol = [gc_all[_brows(b), h:h + 1] for b, h in probs]
    grow = [gc_t[b][h:h + 1, :] for b, h in probs]
    decay = [jnp.where(causal, jnp.exp(jnp.where(causal, gc - gr, 0.0)), 0.0) for gc, gr in zip(gcol, grow)]
    kb = [x * y for x, y in zip(k, beta)]
    raw = [_mm(jnp.concatenate([x, y], axis=0), z, _NT) for x, y, z in zip(kb, q, k)]
    a_mat = [jnp.where(strict, r[:CHUNK] * d, 0.0) for r, d in zip(raw, decay)]
    attn = [r[CHUNK:] * d for r, d in zip(raw, decay)]
    t_mat = _inv_unit_lower(a_mat)
    eg = [jnp.exp(x) for x in gcol]
    sol = [_mm(t, jnp.concatenate([x * y, z * e], axis=1))
           for t, x, y, z, e in zip(t_mat, v, beta, kb, eg)]
    s_old = [s_ref[b, h] for b, h in probs]
    ws = [_mm(jnp.concatenate([x[:, HEAD_D:], y * e], axis=0), s)
          for x, y, e, s in zip(sol, q, eg, s_old)]
    v_new = [x[:, :HEAD_D] - y[:CHUNK] for x, y in zip(sol, ws)]
    o = [y[CHUNK:] + _mm(a, x) for y, a, x in zip(ws, attn, v_new)]
    glast = [x[CHUNK - 1:CHUNK, :] for x in gcol]
    s_new = [s * jnp.exp(gl) + _mm(x * jnp.exp(gl - gc), vn, _TN)
             for s, gl, x, gc, vn in zip(s_old, glast, k, gcol, v_new)]
    for (b, h), s in zip(probs, s_new):
        s_ref[b, h] = s
    for (b, h), x in zip(probs, o):
        z = u_ref[b, :, _hcols(h, 3 * GROUP_W)]
        o_ref[b, :, _hcols(h)] = (_rms(x, norm_w) * _silu(z)).astype(BF16)


def _gdn(u, conv_w, prm):
    bsz, seq, _ = u.shape
    return pl.pallas_call(
        _gdn_kernel,
        out_shape=jax.ShapeDtypeStruct((bsz, seq, GROUP_W), BF16),
        grid=(seq // CHUNK,),
        in_specs=[
            pl.BlockSpec((bsz, CHUNK, 4 * GROUP_W), lambda t: (0, t, GDN_BLK)),
            pl.BlockSpec((bsz, CHUNK, LANES), lambda t: (0, t, GATES_BLK)),
            pl.BlockSpec((CONV_K, 3 * GROUP_W), lambda t: (0, 0)),
            pl.BlockSpec((8, LANES), lambda t: (0, 0)),
        ],
        out_specs=pl.BlockSpec((bsz, CHUNK, GROUP_W), lambda t: (0, t, 0)),
        scratch_shapes=[
            pltpu.VMEM((bsz, HALO + CHUNK, 3 * GROUP_W), F32),
            pltpu.VMEM((bsz, N_HEADS, HEAD_D, HEAD_D), F32),
        ],
        compiler_params=pltpu.CompilerParams(
            dimension_semantics=("arbitrary",), vmem_limit_bytes=VMEM_LIMIT),
        name="gdn",
    )(u, u, conv_w, prm)


def _mlstm_kernel(u_ref, gt_ref, cw_ref, prm_ref, o_ref, buf, c_ref, n_ref, m_ref):
    nb = u_ref.shape[0]

    @pl.when(pl.program_id(0) == 0)
    def _():
        buf[:, 0:HALO, :] = jnp.zeros((nb, HALO, 2 * GROUP_W), F32)
        c_ref[...] = jnp.zeros_like(c_ref)
        n_ref[...] = jnp.zeros_like(n_ref)
        m_ref[...] = jnp.zeros_like(m_ref)

    cw = cw_ref[...]
    qk = _silu(jnp.concatenate(
        [_causal_conv(buf.at[b], u_ref[b, :, 0:2 * GROUP_W], cw) for b in range(nb)], axis=0))
    gates = gt_ref[...].reshape(nb * CHUNK, LANES)
    prm = prm_ref[...]
    pre = gates + prm[0:1, :]
    b_all = _chunk_cumsum(-_softplus(-pre))
    b_tr = [b_all[_brows(b), :].T for b in range(nb)]
    i_tr = [pre[_brows(b), :].T for b in range(nb)]

    ri = lax.broadcasted_iota(jnp.int32, (CHUNK, CHUNK), 0)
    ci = lax.broadcasted_iota(jnp.int32, (CHUNK, CHUNK), 1)
    causal = ri >= ci

    probs = [(b, h) for b in range(nb) for h in range(N_HEADS)]
    li = [2 * N_HEADS + h for _, h in probs]
    lf = [3 * N_HEADS + h for _, h in probs]
    q = [qk[_brows(b), _hcols(h)] for b, h in probs]
    k = [qk[_brows(b), _hcols(h, GROUP_W)] * (HEAD_D ** -0.5) for b, h in probs]
    v = [u_ref[b, :, _hcols(h, 2 * GROUP_W)] for b, h in probs]
    bcol = [b_all[_brows(b), l:l + 1] for (b, _), l in zip(probs, lf)]
    icol = [pre[_brows(b), l:l + 1] for (b, _), l in zip(probs, li)]
    brow = [b_tr[b][l:l + 1, :] for (b, _), l in zip(probs, lf)]
    irow = [i_tr[b][l:l + 1, :] for (b, _), l in zip(probs, li)]
    m_old = [m_ref[b, h][0:1, 0:1] for b, h in probs]
    c_old = [c_ref[b, h] for b, h in probs]
    n_old = [n_ref[b, h][0:1, :] for b, h in probs]

    d = [jnp.where(causal, bc - br + ir, -jnp.inf) for bc, br, ir in zip(bcol, brow, irow)]
    inter = [bc + m for bc, m in zip(bcol, m_old)]
    m_t = [jnp.maximum(x, jnp.max(y, axis=-1, keepdims=True)) for x, y in zip(inter, d)]
    qk_raw = [_mm(x, y, _NT) for x, y in zip(q, k)]
    s = [r * jnp.exp(x - m) for r, x, m in zip(qk_raw, d, m_t)]
    a = [jnp.exp(x - m) for x, m in zip(inter, m_t)]
    qc = [_mm(x, c) for x, c in zip(q, c_old)]
    sv = [_mm(x, y) for x, y in zip(s, v)]
    num = [x * y + z for x, y, z in zip(a, qc, sv)]
    den = [x * jnp.sum(y * n, axis=-1, keepdims=True) + jnp.sum(z, axis=-1, keepdims=True)
           for x, y, n, z in zip(a, q, n_old, s)]
    h_t = [x / jnp.maximum(jnp.abs(y), jnp.exp(-m)) for x, y, m in zip(num, den, m_t)]
    g = [x[CHUNK - 1:CHUNK, :] for x in bcol]
    w_log = [x - bc + ic for x, bc, ic in zip(g, bcol, icol)]
    m_new = [jnp.maximum(x + m, jnp.max(w, axis=0, keepdims=True)) for x, m, w in zip(g, m_old, w_log)]
    scale = [jnp.exp(x + m - mn) for x, m, mn in zip(g, m_old, m_new)]
    kw = [x * jnp.exp(w - mn) for x, w, mn in zip(k, w_log, m_new)]
    c_new = [sc * c + _mm(x, y, _TN) for sc, c, x, y in zip(scale, c_old, kw, v)]
    n_new = [sc * n + jnp.sum(x, axis=0, keepdims=True) for sc, n, x in zip(scale, n_old, kw)]
    for (b, h), c, n, m in zip(probs, c_new, n_new, m_new):
        c_ref[b, h] = c
        n_ref[b, h] = jnp.broadcast_to(n, (8, HEAD_D))
        m_ref[b, h] = jnp.broadcast_to(m, (8, HEAD_D))
    for (b, h), x in zip(probs, h_t):
        o_gate = u_ref[b, :, _hcols(h, 3 * GROUP_W)]
        o_ref[b, :, _hcols(h)] = _rms(_sigmoid(o_gate) * x, prm[1 + h:2 + h, :]).astype(BF16)


def _mlstm(u, conv_w, prm):
    bsz, seq, _ = u.shape
    return pl.pallas_call(
        _mlstm_kernel,
        out_shape=jax.ShapeDtypeStruct((bsz, seq, GROUP_W), BF16),
        grid=(seq // CHUNK,),
        in_specs=[
            pl.BlockSpec((bsz, CHUNK, 4 * GROUP_W), lambda t: (0, t, MLSTM_BLK)),
            pl.BlockSpec((bsz, CHUNK, LANES), lambda t: (0, t, GATES_BLK)),
            pl.BlockSpec((CONV_K, 2 * GROUP_W), lambda t: (0, 0)),
            pl.BlockSpec((8, LANES), lambda t: (0, 0)),
        ],
        out_specs=pl.BlockSpec((bsz, CHUNK, GROUP_W), lambda t: (0, t, 0)),
        scratch_shapes=[
            pltpu.VMEM((bsz, HALO + CHUNK, 2 * GROUP_W), F32),
            pltpu.VMEM((bsz, N_HEADS, HEAD_D, HEAD_D), F32),
            pltpu.VMEM((bsz, N_HEADS, 8, HEAD_D), F32),
            pltpu.VMEM((bsz, N_HEADS, 8, HEAD_D), F32),
        ],
        compiler_params=pltpu.CompilerParams(
            dimension_semantics=("arbitrary",), vmem_limit_bytes=VMEM_LIMIT),
        name="mlstm",
    )(u, u, conv_w, prm)


def _rglru_kernel(u_ref, cw_ref, prm_ref, wa_ref, wx_ref, o_ref, buf, h_ref):
    @pl.when(pl.program_id(1) == 0)
    def _():
        buf[0:HALO, :] = jnp.zeros((HALO, GROUP_W), F32)
        h_ref[...] = jnp.zeros_like(h_ref)

    tt = u_ref.shape[1]
    prm = prm_ref[...]
    xb = _causal_conv(buf, u_ref[0, :, 0:GROUP_W], cw_ref[...]) + prm[0:1, :]
    ra = []
    rx = []
    for n in range(N_HEADS):
        blk = xb[:, _hcols(n)].astype(BF16)
        ra.append(_dot(blk, wa_ref[n]))
        rx.append(_dot(blk, wx_ref[n]))
    r = _sigmoid(jnp.concatenate(ra, axis=1) + prm[1:2, :])
    i = _sigmoid(jnp.concatenate(rx, axis=1) + prm[2:3, :])
    log_a = -RGLRU_C * r * _softplus(-prm[3:4, :])
    a = jnp.exp(log_a)
    th = jnp.tanh(log_a)
    b = jnp.sqrt(-2.0 * th / (1.0 - th)) * (i * xb)
    row = lax.broadcasted_iota(jnp.int32, (tt, GROUP_W), 0) % HALO
    d = 1
    while d < HALO:
        keep = row >= d
        b = jnp.where(keep, a * pltpu.roll(b, d, 0) + b, b)
        a = jnp.where(keep, a * pltpu.roll(a, d, 0), a)
        d *= 2
    carry = h_ref[0:1, :]
    groups = []
    for g in range(tt // HALO):
        hg = b[g * HALO:(g + 1) * HALO, :] + a[g * HALO:(g + 1) * HALO, :] * carry
        groups.append(hg)
        carry = hg[HALO - 1:HALO, :]
    h = jnp.concatenate(groups, axis=0)
    h_ref[...] = jnp.broadcast_to(carry, (8, GROUP_W))
    gate = u_ref[0, :, GROUP_W:2 * GROUP_W]
    gelu = 0.5 * gate * (1.0 + jnp.tanh(0.7978845608028654 * (gate + 0.044715 * gate * gate * gate)))
    o_ref[0] = (h * gelu).astype(BF16)


def _rglru(u, conv_w, prm, w_a, w_x, *, tt=256):
    bsz, seq, _ = u.shape
    return pl.pallas_call(
        _rglru_kernel,
        out_shape=jax.ShapeDtypeStruct((bsz, seq, GROUP_W), BF16),
        grid=(bsz, seq // tt),
        in_specs=[
            pl.BlockSpec((1, tt, 2 * GROUP_W), lambda b, t: (b, t, RGLRU_BLK)),
            pl.BlockSpec((CONV_K, GROUP_W), lambda b, t: (0, 0)),
            pl.BlockSpec((8, GROUP_W), lambda b, t: (0, 0)),
            pl.BlockSpec((N_HEADS, HEAD_D, HEAD_D), lambda b, t: (0, 0, 0)),
            pl.BlockSpec((N_HEADS, HEAD_D, HEAD_D), lambda b, t: (0, 0, 0)),
        ],
        out_specs=pl.BlockSpec((1, tt, GROUP_W), lambda b, t: (b, t, 0)),
        scratch_shapes=[
            pltpu.VMEM((HALO + tt, GROUP_W), F32),
            pltpu.VMEM((8, GROUP_W), F32),
        ],
        compiler_params=pltpu.CompilerParams(
            dimension_semantics=("parallel", "arbitrary"), vmem_limit_bytes=VMEM_LIMIT),
        name="rglru",
    )(u, conv_w, prm, w_a, w_x)


def _pair_sum(x, lo_mask):
    s_lo = jnp.sum(jnp.where(lo_mask, x, 0.0), axis=-1, keepdims=True)
    s_hi = jnp.sum(jnp.where(lo_mask, 0.0, x), axis=-1, keepdims=True)
    return jnp.where(lo_mask, s_lo, s_hi)


def _stack_heads(x, lo_mask):
    return jnp.concatenate([jnp.where(lo_mask, x, 0.0), jnp.where(lo_mask, 0.0, x)], axis=0)


def _rwkv_kernel(u_ref, mu_ref, prm_ref, wlo_ref, g2_ref, o_ref, buf, s_ref):
    nb = u_ref.shape[0]

    @pl.when(pl.program_id(0) == 0)
    def _():
        buf[:, 0:HALO, :] = jnp.zeros((nb, HALO, RWKV_USED), F32)
        s_ref[...] = jnp.zeros_like(s_ref)

    mu = mu_ref[...]
    xs = []
    for b in range(nb):
        x = u_ref[b, :, 0:RWKV_USED]
        buf[b, HALO:HALO + CHUNK, :] = x
        x_prev = buf[b, HALO - 1:HALO - 1 + CHUNK, :]
        buf[b, 0:HALO, :] = x[CHUNK - HALO:CHUNK, :]
        xs.append(x + (x_prev - x) * mu)
    xs = jnp.concatenate(xs, axis=0)
    prm = prm_ref[...]
    r_all = xs[:, 0:GROUP_W]
    k_all = xs[:, GROUP_W:2 * GROUP_W]
    v_all = xs[:, 2 * GROUP_W:3 * GROUP_W]
    lora = xs[:, 3 * GROUP_W:3 * GROUP_W + LANES]
    lane = lax.broadcasted_iota(jnp.int32, lora.shape, 1)
    lora = jnp.where(lane < RWKV_N, jnp.tanh(lora), lora)
    wa = _mm(lora, wlo_ref[...])
    w_raw = -_softplus(-(prm[0:1, :] + wa[:, :GROUP_W])) - 0.5
    lw_all = -jnp.exp(w_raw)
    a_all = _sigmoid(prm[1:2, :] + wa[:, GROUP_W:])
    g_all = _mm(_sigmoid(xs[:, 3 * GROUP_W + LANES:3 * GROUP_W + 2 * LANES]), g2_ref[...])
    kk_all = k_all * prm[2:3, :]
    k2_all = k_all * (1.0 + (a_all - 1.0) * prm[3:4, :])
    lc_all = _chunk_cumsum(lw_all)

    lo_mask = lax.broadcasted_iota(jnp.int32, (CHUNK, LANES), 1) < RWKV_N
    n2 = 2 * CHUNK
    ri = lax.broadcasted_iota(jnp.int32, (n2, n2), 0)
    ci = lax.broadcasted_iota(jnp.int32, (n2, n2), 1)
    same = (ri // CHUNK) == (ci // CHUNK)
    strict = same & ((ci % CHUNK) < (ri % CHUNK))
    incl = same & ((ci % CHUNK) <= (ri % CHUNK))

    probs = [(b, p) for b in range(nb) for p in range(N_PAIRS)]

    def sel(arr):
        return [arr[_brows(b), p * LANES:(p + 1) * LANES] for b, p in probs]

    r, v, k2, a_sig, lw, lc = sel(r_all), sel(v_all), sel(k2_all), sel(a_all), sel(lw_all), sel(lc_all)
    kk = [x * lax.rsqrt(_pair_sum(x * x, lo_mask) + 1e-6) for x in sel(kk_all)]
    w_incl = [jnp.exp(x) for x in lc]
    w_inv = [jnp.exp(-x) for x in lc]
    a_hat = [_stack_heads(-x * jnp.exp(c - w), lo_mask) for x, c, w in zip(kk, lc, lw)]
    b_hat = [_stack_heads(x * y * w, lo_mask) for x, y, w in zip(kk, a_sig, w_inv)]
    k_hat = [_stack_heads(x * w, lo_mask) for x, w in zip(k2, w_inv)]
    r_hat = [_stack_heads(x * w, lo_mask) for x, w in zip(r, w_incl)]
    v_st = [_stack_heads(x, lo_mask) for x in v]
    s_old = [s_ref[b, p] for b, p in probs]

    ar = [jnp.concatenate([x, y], axis=0).astype(BF16) for x, y in zip(a_hat, r_hat)]
    bk = [jnp.concatenate([x, y], axis=0).astype(BF16) for x, y in zip(b_hat, k_hat)]
    cross = [_dot(x, y, _NT) for x, y in zip(ar, bk)]
    a_ab = [jnp.where(strict, x[:n2, :n2], 0.0) for x in cross]
    a_ak = [jnp.where(strict, x[:n2, n2:], 0.0) for x in cross]
    m_r = [jnp.concatenate([jnp.where(incl, x[n2:, :n2], 0.0), jnp.where(incl, x[n2:, n2:], 0.0)], axis=1)
           for x in cross]
    t_mat = _inv_unit_lower([-x for x in a_ab])
    ars = [_dot(x, s.astype(BF16), _NT) for x, s in zip(ar, s_old)]
    pv = [_mm(t, x[:n2] + _mm(y, z)) for t, x, y, z in zip(t_mat, ars, a_ak, v_st)]
    pvv = [jnp.concatenate([x, y], axis=0).astype(BF16) for x, y in zip(pv, v_st)]
    y_st = [x[n2:] + _dot(m.astype(BF16), z) for x, m, z in zip(ars, m_r, pvv)]
    s_new = [(s + _dot(x, y, _TN)) * w[CHUNK - 1:CHUNK, :] for s, x, y, w in zip(s_old, pvv, bk, w_incl)]
    for (b, p), s in zip(probs, s_new):
        s_ref[b, p] = s
    for (b, p), ys, rr, kk2, vv in zip(probs, y_st, r, k2, v):
        cols = slice(p * LANES, (p + 1) * LANES)
        y = ys[:CHUNK, :] + ys[CHUNK:, :]
        yc = y - _pair_sum(y, lo_mask) * (1.0 / RWKV_N)
        var = _pair_sum(yc * yc, lo_mask) * (1.0 / RWKV_N)
        y = yc * lax.rsqrt(var + RWKV_LN_EPS) * prm[5:6, cols] + prm[6:7, cols]
        y = y + _pair_sum(rr * kk2 * prm[4:5, cols], lo_mask) * vv
        o_ref[b, :, cols] = (y * g_all[_brows(b), cols]).astype(BF16)


def _rwkv(u, mu, prm, w_lora, g2):
    bsz, seq, _ = u.shape
    return pl.pallas_call(
        _rwkv_kernel,
        out_shape=jax.ShapeDtypeStruct((bsz, seq, GROUP_W), BF16),
        grid=(seq // CHUNK,),
        in_specs=[
            pl.BlockSpec((bsz, CHUNK, 4 * GROUP_W), lambda t: (0, t, RWKV_BLK)),
            pl.BlockSpec((1, RWKV_USED), lambda t: (0, 0)),
            pl.BlockSpec((8, GROUP_W), lambda t: (0, 0)),
            pl.BlockSpec((LANES, 2 * GROUP_W), lambda t: (0, 0)),
            pl.BlockSpec((LANES, GROUP_W), lambda t: (0, 0)),
        ],
        out_specs=pl.BlockSpec((bsz, CHUNK, GROUP_W), lambda t: (0, t, 0)),
        scratch_shapes=[
            pltpu.VMEM((bsz, HALO + CHUNK, RWKV_USED), F32),
            pltpu.VMEM((bsz, N_PAIRS, LANES, LANES), F32),
        ],
        compiler_params=pltpu.CompilerParams(
            dimension_semantics=("arbitrary",), vmem_limit_bytes=VMEM_LIMIT),
        name="rwkv",
    )(u, mu, prm, w_lora, g2)


def _outproj_kernel(x_ref, y0_ref, y1_ref, y2_ref, y3_ref, w_ref, o_ref):
    acc = x_ref[...]
    for g, y_ref in enumerate((y0_ref, y1_ref, y2_ref, y3_ref)):
        acc = acc + _dot(y_ref[...], w_ref[g * GROUP_W:(g + 1) * GROUP_W, :])
    o_ref[...] = acc


def _outproj(x2d, ys, w_out, *, tm=512):
    n_tok = x2d.shape[0]
    y_spec = pl.BlockSpec((tm, GROUP_W), lambda i: (i, 0))
    return pl.pallas_call(
        _outproj_kernel,
        out_shape=jax.ShapeDtypeStruct((n_tok, D_MODEL), F32),
        grid=(n_tok // tm,),
        in_specs=[pl.BlockSpec((tm, D_MODEL), lambda i: (i, 0)), y_spec, y_spec, y_spec, y_spec,
                  pl.BlockSpec((D_MODEL, D_MODEL), lambda i: (0, 0))],
        out_specs=pl.BlockSpec((tm, D_MODEL), lambda i: (i, 0)),
        compiler_params=pltpu.CompilerParams(
            dimension_semantics=("parallel",), vmem_limit_bytes=VMEM_LIMIT),
        name="outproj",
    )(x2d, *ys, w_out)


def _ffn_kernel(x_ref, nw_ref, wg_ref, wu_ref, wd_ref, fw_ref, o_ref, n_scr, *, n_f, final):
    f = pl.program_id(1)

    @pl.when(f == 0)
    def _():
        x = x_ref[...]
        n_scr[...] = _rms(x, nw_ref[...]).astype(BF16)
        o_ref[...] = x

    n = n_scr[...]
    gate = jnp.dot(n, wg_ref[...], preferred_element_type=F32)
    up = jnp.dot(n, wu_ref[...], preferred_element_type=F32)
    hid = (_silu(gate) * up).astype(BF16)
    o_ref[...] += jnp.dot(hid, wd_ref[...], preferred_element_type=F32)

    if final:
        @pl.when(f == n_f - 1)
        def _():
            o_ref[...] = _rms(o_ref[...], fw_ref[...])


def _ffn(x2d, norm_w, w_gate, w_up, w_down, final_w, *, final, tm=1024, tf=512):
    n_tok = x2d.shape[0]
    n_f = FFN_HIDDEN // tf
    return pl.pallas_call(
        functools.partial(_ffn_kernel, n_f=n_f, final=final),
        out_shape=jax.ShapeDtypeStruct((n_tok, D_MODEL), F32),
        grid=(n_tok // tm, n_f),
        in_specs=[
            pl.BlockSpec((tm, D_MODEL), lambda i, f: (i, 0)),
            pl.BlockSpec((1, D_MODEL), lambda i, f: (0, 0)),
            pl.BlockSpec((D_MODEL, tf), lambda i, f: (0, f)),
            pl.BlockSpec((D_MODEL, tf), lambda i, f: (0, f)),
            pl.BlockSpec((tf, D_MODEL), lambda i, f: (f, 0)),
            pl.BlockSpec((1, D_MODEL), lambda i, f: (0, 0)),
        ],
        out_specs=pl.BlockSpec((tm, D_MODEL), lambda i, f: (i, 0)),
        scratch_shapes=[pltpu.VMEM((tm, D_MODEL), BF16)],
        compiler_params=pltpu.CompilerParams(
            dimension_semantics=("parallel", "arbitrary"), vmem_limit_bytes=VMEM_LIMIT),
        name="ffn_final" if final else "ffn",
    )(x2d, norm_w.reshape(1, D_MODEL), w_gate, w_up, w_down, final_w.reshape(1, D_MODEL))


def _pad_lanes(v, offset, width):
    v = v.astype(F32)
    return jnp.concatenate([jnp.zeros((offset,), F32), v, jnp.zeros((width - offset - v.shape[0],), F32)])


def _rows(rows, width):
    rows = [r.astype(F32).reshape(width) for r in rows]
    return jnp.stack(rows + [jnp.zeros((width,), F32)] * (8 - len(rows)))


def _perm_kernel(w_ref, o_ref):
    o_ref[:, 0:2048] = w_ref[:, 0:2048].astype(BF16)
    o_ref[:, 2048:4096] = w_ref[:, 2056:4104].astype(BF16)
    o_ref[:, 4096:4096 + RWKV_USED] = w_ref[:, 5136:6928].astype(BF16)
    lane = lax.broadcasted_iota(jnp.int32, (w_ref.shape[0], LANES), 1)
    gdn_ab = w_ref[:, 2048:2176]
    ml_if = w_ref[:, 4096:4224]
    gates = jnp.where(lane < 8, gdn_ab, jnp.where(lane < 16, ml_if, 0.0))
    o_ref[:, GATES_BLK * LANES:(GATES_BLK + 1) * LANES] = gates.astype(BF16)
    o_ref[:, (GATES_BLK + 1) * LANES:(GATES_BLK + 2) * LANES] = jnp.zeros((w_ref.shape[0], LANES), BF16)
    o_ref[:, RGLRU_BLK * 1024:U_COLS] = w_ref[:, 4112:5136].astype(BF16)


def _perm_w_in(w, *, tr=256):
    d, n_in = w.shape
    return pl.pallas_call(
        _perm_kernel,
        out_shape=jax.ShapeDtypeStruct((d, U_COLS), BF16),
        grid=(d // tr,),
        in_specs=[pl.BlockSpec((tr, n_in), lambda i: (i, 0))],
        out_specs=pl.BlockSpec((tr, U_COLS), lambda i: (i, 0)),
        compiler_params=pltpu.CompilerParams(
            dimension_semantics=("parallel",), vmem_limit_bytes=VMEM_LIMIT),
        name="perm_w_in",
    )(w)


def kernel(x, attn_norm, w_in, gdn_conv, gdn_a_log, gdn_dt_bias, gdn_norm, mlstm_conv, mlstm_b_i, mlstm_b_f, mlstm_norm, rglru_conv, rglru_conv_b, rglru_w_a, rglru_b_a, rglru_w_x, rglru_b_x, rglru_lambda, rwkv_mu, rwkv_w0, rwkv_w2, rwkv_a0, rwkv_a2, rwkv_g2, rwkv_k_k, rwkv_k_a, rwkv_r_k, rwkv_ln_w, rwkv_ln_b, w_out, ffn_norm, ffn_w_gate, ffn_w_up, ffn_w_down, final_norm):
    bsz, seq, d = x.shape
    depth = w_in.shape[0]
    x2d = x.reshape(bsz * seq, d)
    for l in range(depth):
        u = _inproj(x2d, attn_norm[l], _perm_w_in(w_in[l])).reshape(bsz, seq, U_COLS)

        gdn_prm = _rows([_pad_lanes(gdn_a_log[l], 0, LANES), _pad_lanes(gdn_dt_bias[l], 0, LANES),
                         gdn_norm[l]], LANES)
        y_gdn = _gdn(u, gdn_conv[l], gdn_prm)

        ml_bias = _pad_lanes(mlstm_b_i[l], 2 * N_HEADS, LANES) + _pad_lanes(mlstm_b_f[l], 3 * N_HEADS, LANES)
        ml_prm = _rows([ml_bias] + [mlstm_norm[l, h] for h in range(N_HEADS)], LANES)
        y_ml = _mlstm(u, mlstm_conv[l], ml_prm)

        rg_prm = _rows([rglru_conv_b[l], rglru_b_a[l], rglru_b_x[l], rglru_lambda[l]], GROUP_W)
        y_rg = _rglru(u, rglru_conv[l], rg_prm, rglru_w_a[l].astype(BF16), rglru_w_x[l].astype(BF16))

        rw_prm = _rows([rwkv_w0[l], rwkv_a0[l], rwkv_k_k[l], rwkv_k_a[l], rwkv_r_k[l],
                        rwkv_ln_w[l], rwkv_ln_b[l]], GROUP_W)
        zeros_lora = jnp.zeros((RWKV_N, GROUP_W), F32)
        w_lora = jnp.concatenate([
            jnp.concatenate([rwkv_w2[l], zeros_lora], axis=1),
            jnp.concatenate([zeros_lora, rwkv_a2[l]], axis=1)], axis=0).astype(BF16)
        y_rw = _rwkv(u, rwkv_mu[l].reshape(1, RWKV_USED), rw_prm, w_lora, rwkv_g2[l].astype(BF16))

        ys = [y.reshape(bsz * seq, GROUP_W) for y in (y_gdn, y_ml, y_rg, y_rw)]
        x2d = _outproj(x2d, ys, w_out[l].astype(BF16))
        x2d = _ffn(x2d, ffn_norm[l], ffn_w_gate[l].astype(BF16), ffn_w_up[l].astype(BF16),
                   ffn_w_down[l].astype(BF16), final_norm, final=(l == depth - 1))
    return x2d.reshape(bsz, seq, d)
```

```python
import functools

import jax
import jax.numpy as jnp
from jax import lax
from jax.experimental import pallas as pl
from jax.experimental.pallas import tpu as pltpu

F32 = jnp.float32
BF16 = jnp.bfloat16

D_MODEL = 2048
GROUP_W = 512
HEAD_D = 128
N_HEADS = 4
RWKV_N = 64
N_PAIRS = GROUP_W // (2 * RWKV_N)
CONV_K = 4
CHUNK = 64
FFN_HIDDEN = 5632
NORM_EPS = 1e-6
RWKV_LN_EPS = 64e-5
RGLRU_C = 8.0
LANES = 128
HALO = 8
INV_LEVELS = 5

U_COLS = 7168
GDN_BLK = 0
MLSTM_BLK = 1
RWKV_BLK = 2
GATES_BLK = 46
RGLRU_BLK = 6
RWKV_USED = 1792

VMEM_LIMIT = 60 * 1024 * 1024

_NN = (((1,), (0,)), ((), ()))
_NT = (((1,), (1,)), ((), ()))
_TN = (((0,), (0,)), ((), ()))


def _dot(a, b, dims=_NN):
    return lax.dot_general(a, b, dims, preferred_element_type=F32)


def _mm(a, b, dims=_NN):
    return _dot(a.astype(BF16), b.astype(BF16), dims)


def _sigmoid(x):
    return 1.0 / (1.0 + jnp.exp(-x))


def _softplus(x):
    return jnp.maximum(x, 0.0) + jnp.log1p(jnp.exp(-jnp.abs(x)))


def _silu(x):
    return x * _sigmoid(x)


def _rms(x, w, eps=NORM_EPS):
    return x * lax.rsqrt(jnp.mean(x * x, axis=-1, keepdims=True) + eps) * w


def _chunk_cumsum(x):
    row = lax.broadcasted_iota(jnp.int32, x.shape, 0) % CHUNK
    d = 1
    while d < CHUNK:
        x = x + jnp.where(row >= d, pltpu.roll(x, d, 0), 0.0)
        d *= 2
    return x


def _inv_unit_lower(a_list):
    n = a_list[0].shape[0]
    eye = (lax.broadcasted_iota(jnp.int32, (n, n), 0) == lax.broadcasted_iota(jnp.int32, (n, n), 1)).astype(F32)
    inv = [eye - a for a in a_list]
    p = [(-a).astype(BF16) for a in a_list]
    for _ in range(INV_LEVELS):
        p = [_dot(x, x).astype(BF16) for x in p]
        inv = [x + _dot(x.astype(BF16), y) for x, y in zip(inv, p)]
    return inv


def _causal_conv(buf, x, conv_w):
    tt = x.shape[0]
    buf[HALO:HALO + tt, :] = x
    xx = buf[...]
    acc = conv_w[0:1, :] * xx
    for j in range(1, CONV_K):
        acc = conv_w[j:j + 1, :] * xx + pltpu.roll(acc, 1, 0)
    buf[0:HALO, :] = x[tt - HALO:tt, :]
    return acc[HALO:, :]


def _brows(b):
    return slice(b * CHUNK, (b + 1) * CHUNK)


def _hcols(h, base=0):
    return slice(base + h * HEAD_D, base + (h + 1) * HEAD_D)


def _inproj_kernel(x_ref, nw_ref, w_ref, o_ref, n_scr):
    @pl.when(pl.program_id(1) == 0)
    def _():
        n_scr[...] = _rms(x_ref[...], nw_ref[...]).astype(BF16)

    o_ref[...] = jnp.dot(n_scr[...], w_ref[...], preferred_element_type=F32)


def _inproj(x2d, norm_w, w_perm, layer, *, tm=1024, tn=1024):
    n_tok = x2d.shape[0]
    return pl.pallas_call(
        _inproj_kernel,
        out_shape=jax.ShapeDtypeStruct((n_tok, U_COLS), F32),
        grid=(n_tok // tm, U_COLS // tn),
        in_specs=[
            pl.BlockSpec((tm, D_MODEL), lambda i, j: (i, 0)),
            pl.BlockSpec((1, D_MODEL), lambda i, j: (0, 0)),
            pl.BlockSpec((None, D_MODEL, tn), lambda i, j: (layer, 0, j)),
        ],
        out_specs=pl.BlockSpec((tm, tn), lambda i, j: (i, j)),
        scratch_shapes=[pltpu.VMEM((tm, D_MODEL), BF16)],
        compiler_params=pltpu.CompilerParams(
            dimension_semantics=("parallel", "arbitrary"), vmem_limit_bytes=VMEM_LIMIT),
        name="inproj",
    )(x2d, norm_w.reshape(1, D_MODEL), w_perm)


def _gdn_kernel(u_ref, gt_ref, cw_ref, prm_ref, o_ref, buf, s_ref):
    nb = u_ref.shape[0]

    @pl.when(pl.program_id(0) == 0)
    def _():
        buf[:, 0:HALO, :] = jnp.zeros((nb, HALO, 3 * GROUP_W), F32)
        s_ref[...] = jnp.zeros_like(s_ref)

    cw = cw_ref[...]
    qkv = _silu(jnp.concatenate(
        [_causal_conv(buf.at[b], u_ref[b, :, 0:3 * GROUP_W], cw) for b in range(nb)], axis=0))
    gates = gt_ref[...].reshape(nb * CHUNK, LANES)
    prm = prm_ref[...]
    g_all = -jnp.exp(prm[0:1, :]) * _softplus(gates + prm[1:2, :])
    beta_all = _sigmoid(gates)
    gc_all = _chunk_cumsum(g_all)
    norm_w = prm[2:3, :]
    gc_t = [gc_all[_brows(b), :].T for b in range(nb)]

    qn, kn = [], []
    for h in range(N_HEADS):
        q = qkv[:, _hcols(h)]
        k = qkv[:, _hcols(h, GROUP_W)]
        qn.append(q * (lax.rsqrt(jnp.sum(q * q, axis=-1, keepdims=True) + 1e-6) * (HEAD_D ** -0.5)))
        kn.append(k * lax.rsqrt(jnp.sum(k * k, axis=-1, keepdims=True) + 1e-6))

    ri = lax.broadcasted_iota(jnp.int32, (CHUNK, CHUNK), 0)
    ci = lax.broadcasted_iota(jnp.int32, (CHUNK, CHUNK), 1)
    causal = ri >= ci
    strict = ri > ci

    probs = [(b, h) for b in range(nb) for h in range(N_HEADS)]
    q = [qn[h][_brows(b), :] for b, h in probs]
    k = [kn[h][_brows(b), :] for b, h in probs]
    v = [qkv[_brows(b), _hcols(h, 2 * GROUP_W)] for b, h in probs]
    beta = [beta_all[_brows(b), N_HEADS + h:N_HEADS + h + 1] for b, h in probs]
    gcol = [gc_all[_brows(b), h:h + 1] for b, h in probs]
    grow = [gc_t[b][h:h + 1, :] for b, h in probs]
    decay = [jnp.where(causal, jnp.exp(jnp.where(causal, gc - gr, 0.0)), 0.0) for gc, gr in zip(gcol, grow)]
    kb = [x * y for x, y in zip(k, beta)]
    raw = [_mm(jnp.concatenate([x, y], axis=0), z, _NT) for x, y, z in zip(kb, q, k)]
    a_mat = [jnp.where(strict, r[:CHUNK] * d, 0.0) for r, d in zip(raw, decay)]
    attn = [r[CHUNK:] * d for r, d in zip(raw, decay)]
    t_mat = _inv_unit_lower(a_mat)
    eg = [jnp.exp(x) for x in gcol]
    sol = [_mm(t, jnp.concatenate([x * y, z * e], axis=1))
           for t, x, y, z, e in zip(t_mat, v, beta, kb, eg)]
    s_old = [s_ref[b, h] for b, h in probs]
    ws = [_mm(jnp.concatenate([x[:, HEAD_D:], y * e], axis=0), s)
          for x, y, e, s in zip(sol, q, eg, s_old)]
    v_new = [x[:, :HEAD_D] - y[:CHUNK] for x, y in zip(sol, ws)]
    o = [y[CHUNK:] + _mm(a, x) for y, a, x in zip(ws, attn, v_new)]
    glast = [x[CHUNK - 1:CHUNK, :] for x in gcol]
    s_new = [s * jnp.exp(gl) + _mm(x * jnp.exp(gl - gc), vn, _TN)
             for s, gl, x, gc, vn in zip(s_old, glast, k, gcol, v_new)]
    for (b, h), s in zip(probs, s_new):
        s_ref[b, h] = s
    for (b, h), x in zip(probs, o):
        z = u_ref[b, :, _hcols(h, 3 * GROUP_W)]
        o_ref[b, :, _hcols(h)] = (_rms(x, norm_w) * _silu(z)).astype(BF16)


def _gdn(u, conv_w, prm):
    bsz, seq, _ = u.shape
    return pl.pallas_call(
        _gdn_kernel,
        out_shape=jax.ShapeDtypeStruct((bsz, seq, GROUP_W), BF16),
        grid=(seq // CHUNK,),
        in_specs=[
            pl.BlockSpec((bsz, CHUNK, 4 * GROUP_W), lambda t: (0, t, GDN_BLK)),
            pl.BlockSpec((bsz, CHUNK, LANES), lambda t: (0, t, GATES_BLK)),
            pl.BlockSpec((CONV_K, 3 * GROUP_W), lambda t: (0, 0)),
            pl.BlockSpec((8, LANES), lambda t: (0, 0)),
        ],
        out_specs=pl.BlockSpec((bsz, CHUNK, GROUP_W), lambda t: (0, t, 0)),
        scratch_shapes=[
            pltpu.VMEM((bsz, HALO + CHUNK, 3 * GROUP_W), F32),
            pltpu.VMEM((bsz, N_HEADS, HEAD_D, HEAD_D), F32),
        ],
        compiler_params=pltpu.CompilerParams(
            dimension_semantics=("arbitrary",), vmem_limit_bytes=VMEM_LIMIT),
        name="gdn",
    )(u, u, conv_w, prm)


def _mlstm_kernel(u_ref, gt_ref, cw_ref, prm_ref, o_ref, buf, c_ref, n_ref, m_ref):
    nb = u_ref.shape[0]

    @pl.when(pl.program_id(0) == 0)
    def _():
        buf[:, 0:HALO, :] = jnp.zeros((nb, HALO, 2 * GROUP_W), F32)
        c_ref[...] = jnp.zeros_like(c_ref)
        n_ref[...] = jnp.zeros_like(n_ref)
        m_ref[...] = jnp.zeros_like(m_ref)

    cw = cw_ref[...]
    qk = _silu(jnp.concatenate(
        [_causal_conv(buf.at[b], u_ref[b, :, 0:2 * GROUP_W], cw) for b in range(nb)], axis=0))
    gates = gt_ref[...].reshape(nb * CHUNK, LANES)
    prm = prm_ref[...]
    pre = gates + prm[0:1, :]
    b_all = _chunk_cumsum(-_softplus(-pre))
    b_tr = [b_all[_brows(b), :].T for b in range(nb)]
    i_tr = [pre[_brows(b), :].T for b in range(nb)]

    ri = lax.broadcasted_iota(jnp.int32, (CHUNK, CHUNK), 0)
    ci = lax.broadcasted_iota(jnp.int32, (CHUNK, CHUNK), 1)
    causal = ri >= ci

    probs = [(b, h) for b in range(nb) for h in range(N_HEADS)]
    li = [2 * N_HEADS + h for _, h in probs]
    lf = [3 * N_HEADS + h for _, h in probs]
    q = [qk[_brows(b), _hcols(h)] for b, h in probs]
    k = [qk[_brows(b), _hcols(h, GROUP_W)] * (HEAD_D ** -0.5) for b, h in probs]
    v = [u_ref[b, :, _hcols(h, 2 * GROUP_W)] for b, h in probs]
    bcol = [b_all[_brows(b), l:l + 1] for (b, _), l in zip(probs, lf)]
    icol = [pre[_brows(b), l:l + 1] for (b, _), l in zip(probs, li)]
    brow = [b_tr[b][l:l + 1, :] for (b, _), l in zip(probs, lf)]
    irow = [i_tr[b][l:l + 1, :] for (b, _), l in zip(probs, li)]
    m_old = [m_ref[b, h][0:1, 0:1] for b, h in probs]
    c_old = [c_ref[b, h] for b, h in probs]
    n_old = [n_ref[b, h][0:1, :] for b, h in probs]

    d = [jnp.where(causal, bc - br + ir, -jnp.inf) for bc, br, ir in zip(bcol, brow, irow)]
    inter = [bc + m for bc, m in zip(bcol, m_old)]
    m_t = [jnp.maximum(x, jnp.max(y, axis=-1, keepdims=True)) for x, y in zip(inter, d)]
    qk_raw = [_mm(x, y, _NT) for x, y in zip(q, k)]
    s = [r * jnp.exp(x - m) for r, x, m in zip(qk_raw, d, m_t)]
    a = [jnp.exp(x - m) for x, m in zip(inter, m_t)]
    qc = [_mm(x, c) for x, c in zip(q, c_old)]
    sv = [_mm(x, y) for x, y in zip(s, v)]
    num = [x * y + z for x, y, z in zip(a, qc, sv)]
    den = [x * jnp.sum(y * n, axis=-1, keepdims=True) + jnp.sum(z, axis=-1, keepdims=True)
           for x, y, n, z in zip(a, q, n_old, s)]
    h_t = [x / jnp.maximum(jnp.abs(y), jnp.exp(-m)) for x, y, m in zip(num, den, m_t)]
    g = [x[CHUNK - 1:CHUNK, :] for x in bcol]
    w_log = [x - bc + ic for x, bc, ic in zip(g, bcol, icol)]
    m_new = [jnp.maximum(x + m, jnp.max(w, axis=0, keepdims=True)) for x, m, w in zip(g, m_old, w_log)]
    scale = [jnp.exp(x + m - mn) for x, m, mn in zip(g, m_old, m_new)]
    kw = [x * jnp.exp(w - mn) for x, w, mn in zip(k, w_log, m_new)]
    c_new = [sc * c + _mm(x, y, _TN) for sc, c, x, y in zip(scale, c_old, kw, v)]
    n_new = [sc * n + jnp.sum(x, axis=0, keepdims=True) for sc, n, x in zip(scale, n_old, kw)]
    for (b, h), c, n, m in zip(probs, c_new, n_new, m_new):
        c_ref[b, h] = c
        n_ref[b, h] = jnp.broadcast_to(n, (8, HEAD_D))
        m_ref[b, h] = jnp.broadcast_to(m, (8, HEAD_D))
    for (b, h), x in zip(probs, h_t):
        o_gate = u_ref[b, :, _hcols(h, 3 * GROUP_W)]
        o_ref[b, :, _hcols(h)] = _rms(_sigmoid(o_gate) * x, prm[1 + h:2 + h, :]).astype(BF16)


def _mlstm(u, conv_w, prm):
    bsz, seq, _ = u.shape
    return pl.pallas_call(
        _mlstm_kernel,
        out_shape=jax.ShapeDtypeStruct((bsz, seq, GROUP_W), BF16),
        grid=(seq // CHUNK,),
        in_specs=[
            pl.BlockSpec((bsz, CHUNK, 4 * GROUP_W), lambda t: (0, t, MLSTM_BLK)),
            pl.BlockSpec((bsz, CHUNK, LANES), lambda t: (0, t, GATES_BLK)),
            pl.BlockSpec((CONV_K, 2 * GROUP_W), lambda t: (0, 0)),
            pl.BlockSpec((8, LANES), lambda t: (0, 0)),
        ],
        out_specs=pl.BlockSpec((bsz, CHUNK, GROUP_W), lambda t: (0, t, 0)),
        scratch_shapes=[
            pltpu.VMEM((bsz, HALO + CHUNK, 2 * GROUP_W), F32),
            pltpu.VMEM((bsz, N_HEADS, HEAD_D, HEAD_D), F32),
            pltpu.VMEM((bsz, N_HEADS, 8, HEAD_D), F32),
            pltpu.VMEM((bsz, N_HEADS, 8, HEAD_D), F32),
        ],
        compiler_params=pltpu.CompilerParams(
            dimension_semantics=("arbitrary",), vmem_limit_bytes=VMEM_LIMIT),
        name="mlstm",
    )(u, u, conv_w, prm)


def _rglru_kernel(u_ref, cw_ref, prm_ref, wa_ref, wx_ref, o_ref, buf, h_ref):
    @pl.when(pl.program_id(1) == 0)
    def _():
        buf[0:HALO, :] = jnp.zeros((HALO, GROUP_W), F32)
        h_ref[...] = jnp.zeros_like(h_ref)

    tt = u_ref.shape[1]
    prm = prm_ref[...]
    xb = _causal_conv(buf, u_ref[0, :, 0:GROUP_W], cw_ref[...]) + prm[0:1, :]
    ra = []
    rx = []
    for n in range(N_HEADS):
        blk = xb[:, _hcols(n)].astype(BF16)
        ra.append(_dot(blk, wa_ref[n]))
        rx.append(_dot(blk, wx_ref[n]))
    r = _sigmoid(jnp.concatenate(ra, axis=1) + prm[1:2, :])
    i = _sigmoid(jnp.concatenate(rx, axis=1) + prm[2:3, :])
    log_a = -RGLRU_C * r * _softplus(-prm[3:4, :])
    a = jnp.exp(log_a)
    th = jnp.tanh(log_a)
    b = jnp.sqrt(-2.0 * th / (1.0 - th)) * (i * xb)
    row = lax.broadcasted_iota(jnp.int32, (tt, GROUP_W), 0) % HALO
    d = 1
    while d < HALO:
        keep = row >= d
        b = jnp.where(keep, a * pltpu.roll(b, d, 0) + b, b)
        a = jnp.where(keep, a * pltpu.roll(a, d, 0), a)
        d *= 2
    carry = h_ref[0:1, :]
    groups = []
    for g in range(tt // HALO):
        hg = b[g * HALO:(g + 1) * HALO, :] + a[g * HALO:(g + 1) * HALO, :] * carry
        groups.append(hg)
        carry = hg[HALO - 1:HALO, :]
    h = jnp.concatenate(groups, axis=0)
    h_ref[...] = jnp.broadcast_to(carry, (8, GROUP_W))
    gate = u_ref[0, :, GROUP_W:2 * GROUP_W]
    gelu = 0.5 * gate * (1.0 + jnp.tanh(0.7978845608028654 * (gate + 0.044715 * gate * gate * gate)))
    o_ref[0] = (h * gelu).astype(BF16)


def _rglru(u, conv_w, prm, w_a, w_x, *, tt=256):
    bsz, seq, _ = u.shape
    return pl.pallas_call(
        _rglru_kernel,
        out_shape=jax.ShapeDtypeStruct((bsz, seq, GROUP_W), BF16),
        grid=(bsz, seq // tt),
        in_specs=[
            pl.BlockSpec((1, tt, 2 * GROUP_W), lambda b, t: (b, t, RGLRU_BLK)),
            pl.BlockSpec((CONV_K, GROUP_W), lambda b, t: (0, 0)),
            pl.BlockSpec((8, GROUP_W), lambda b, t: (0, 0)),
            pl.BlockSpec((N_HEADS, HEAD_D, HEAD_D), lambda b, t: (0, 0, 0)),
            pl.BlockSpec((N_HEADS, HEAD_D, HEAD_D), lambda b, t: (0, 0, 0)),
        ],
        out_specs=pl.BlockSpec((1, tt, GROUP_W), lambda b, t: (b, t, 0)),
        scratch_shapes=[
            pltpu.VMEM((HALO + tt, GROUP_W), F32),
            pltpu.VMEM((8, GROUP_W), F32),
        ],
        compiler_params=pltpu.CompilerParams(
            dimension_semantics=("parallel", "arbitrary"), vmem_limit_bytes=VMEM_LIMIT),
        name="rglru",
    )(u, conv_w, prm, w_a, w_x)


def _pair_sum(x, lo_mask):
    s_lo = jnp.sum(jnp.where(lo_mask, x, 0.0), axis=-1, keepdims=True)
    s_hi = jnp.sum(jnp.where(lo_mask, 0.0, x), axis=-1, keepdims=True)
    return jnp.where(lo_mask, s_lo, s_hi)


def _stack_heads(x, lo_mask):
    return jnp.concatenate([jnp.where(lo_mask, x, 0.0), jnp.where(lo_mask, 0.0, x)], axis=0)


def _rwkv_kernel(u_ref, mu_ref, prm_ref, wlo_ref, g2_ref, o_ref, buf, s_ref):
    nb = u_ref.shape[0]

    @pl.when(pl.program_id(0) == 0)
    def _():
        buf[:, 0:HALO, :] = jnp.zeros((nb, HALO, RWKV_USED), F32)
        s_ref[...] = jnp.zeros_like(s_ref)

    mu = mu_ref[...]
    xs = []
    for b in range(nb):
        x = u_ref[b, :, 0:RWKV_USED]
        buf[b, HALO:HALO + CHUNK, :] = x
        x_prev = buf[b, HALO - 1:HALO - 1 + CHUNK, :]
        buf[b, 0:HALO, :] = x[CHUNK - HALO:CHUNK, :]
        xs.append(x + (x_prev - x) * mu)
    xs = jnp.concatenate(xs, axis=0)
    prm = prm_ref[...]
    r_all = xs[:, 0:GROUP_W]
    k_all = xs[:, GROUP_W:2 * GROUP_W]
    v_all = xs[:, 2 * GROUP_W:3 * GROUP_W]
    lora = xs[:, 3 * GROUP_W:3 * GROUP_W + LANES]
    lane = lax.broadcasted_iota(jnp.int32, lora.shape, 1)
    lora = jnp.where(lane < RWKV_N, jnp.tanh(lora), lora)
    wa = _mm(lora, wlo_ref[...])
    w_raw = -_softplus(-(prm[0:1, :] + wa[:, :GROUP_W])) - 0.5
    lw_all = -jnp.exp(w_raw)
    a_all = _sigmoid(prm[1:2, :] + wa[:, GROUP_W:])
    g_all = _mm(_sigmoid(xs[:, 3 * GROUP_W + LANES:3 * GROUP_W + 2 * LANES]), g2_ref[...])
    kk_all = k_all * prm[2:3, :]
    k2_all = k_all * (1.0 + (a_all - 1.0) * prm[3:4, :])
    lc_all = _chunk_cumsum(lw_all)

    lo_mask = lax.broadcasted_iota(jnp.int32, (CHUNK, LANES), 1) < RWKV_N
    n2 = 2 * CHUNK
    ri = lax.broadcasted_iota(jnp.int32, (n2, n2), 0)
    ci = lax.broadcasted_iota(jnp.int32, (n2, n2), 1)
    same = (ri // CHUNK) == (ci // CHUNK)
    strict = same & ((ci % CHUNK) < (ri % CHUNK))
    incl = same & ((ci % CHUNK) <= (ri % CHUNK))

    probs = [(b, p) for b in range(nb) for p in range(N_PAIRS)]

    def sel(arr):
        return [arr[_brows(b), p * LANES:(p + 1) * LANES] for b, p in probs]

    r, v, k2, a_sig, lw, lc = sel(r_all), sel(v_all), sel(k2_all), sel(a_all), sel(lw_all), sel(lc_all)
    kk = [x * lax.rsqrt(_pair_sum(x * x, lo_mask) + 1e-6) for x in sel(kk_all)]
    w_incl = [jnp.exp(x) for x in lc]
    w_inv = [jnp.exp(-x) for x in lc]
    a_hat = [_stack_heads(-x * jnp.exp(c - w), lo_mask) for x, c, w in zip(kk, lc, lw)]
    b_hat = [_stack_heads(x * y * w, lo_mask) for x, y, w in zip(kk, a_sig, w_inv)]
    k_hat = [_stack_heads(x * w, lo_mask) for x, w in zip(k2, w_inv)]
    r_hat = [_stack_heads(x * w, lo_mask) for x, w in zip(r, w_incl)]
    v_st = [_stack_heads(x, lo_mask) for x in v]
    s_old = [s_ref[b, p] for b, p in probs]

    ar = [jnp.concatenate([x, y], axis=0).astype(BF16) for x, y in zip(a_hat, r_hat)]
    bk = [jnp.concatenate([x, y], axis=0).astype(BF16) for x, y in zip(b_hat, k_hat)]
    cross = [_dot(x, y, _NT) for x, y in zip(ar, bk)]
    a_ab = [jnp.where(strict, x[:n2, :n2], 0.0) for x in cross]
    a_ak = [jnp.where(strict, x[:n2, n2:], 0.0) for x in cross]
    m_r = [jnp.concatenate([jnp.where(incl, x[n2:, :n2], 0.0), jnp.where(incl, x[n2:, n2:], 0.0)], axis=1)
           for x in cross]
    t_mat = _inv_unit_lower([-x for x in a_ab])
    ars = [_dot(x, s.astype(BF16), _NT) for x, s in zip(ar, s_old)]
    pv = [_mm(t, x[:n2] + _mm(y, z)) for t, x, y, z in zip(t_mat, ars, a_ak, v_st)]
    pvv = [jnp.concatenate([x, y], axis=0).astype(BF16) for x, y in zip(pv, v_st)]
    y_st = [x[n2:] + _dot(m.astype(BF16), z) for x, m, z in zip(ars, m_r, pvv)]
    s_new = [(s + _dot(x, y, _TN)) * w[CHUNK - 1:CHUNK, :] for s, x, y, w in zip(s_old, pvv, bk, w_incl)]
    for (b, p), s in zip(probs, s_new):
        s_ref[b, p] = s
    for (b, p), ys, rr, kk2, vv in zip(probs, y_st, r, k2, v):
        cols = slice(p * LANES, (p + 1) * LANES)
        y = ys[:CHUNK, :] + ys[CHUNK:, :]
        yc = y - _pair_sum(y, lo_mask) * (1.0 / RWKV_N)
        var = _pair_sum(yc * yc, lo_mask) * (1.0 / RWKV_N)
        y = yc * lax.rsqrt(var + RWKV_LN_EPS) * prm[5:6, cols] + prm[6:7, cols]
        y = y + _pair_sum(rr * kk2 * prm[4:5, cols], lo_mask) * vv
        o_ref[b, :, cols] = (y * g_all[_brows(b), cols]).astype(BF16)


def _rwkv(u, mu, prm, w_lora, g2):
    bsz, seq, _ = u.shape
    return pl.pallas_call(
        _rwkv_kernel,
        out_shape=jax.ShapeDtypeStruct((bsz, seq, GROUP_W), BF16),
        grid=(seq // CHUNK,),
        in_specs=[
            pl.BlockSpec((bsz, CHUNK, 4 * GROUP_W), lambda t: (0, t, RWKV_BLK)),
            pl.BlockSpec((1, RWKV_USED), lambda t: (0, 0)),
            pl.BlockSpec((8, GROUP_W), lambda t: (0, 0)),
            pl.BlockSpec((LANES, 2 * GROUP_W), lambda t: (0, 0)),
            pl.BlockSpec((LANES, GROUP_W), lambda t: (0, 0)),
        ],
        out_specs=pl.BlockSpec((bsz, CHUNK, GROUP_W), lambda t: (0, t, 0)),
        scratch_shapes=[
            pltpu.VMEM((bsz, HALO + CHUNK, RWKV_USED), F32),
            pltpu.VMEM((bsz, N_PAIRS, LANES, LANES), F32),
        ],
        compiler_params=pltpu.CompilerParams(
            dimension_semantics=("arbitrary",), vmem_limit_bytes=VMEM_LIMIT),
        name="rwkv",
    )(u, mu, prm, w_lora, g2)


def _outproj_kernel(x_ref, y0_ref, y1_ref, y2_ref, y3_ref, w_ref, o_ref):
    acc = x_ref[...]
    for g, y_ref in enumerate((y0_ref, y1_ref, y2_ref, y3_ref)):
        acc = acc + _dot(y_ref[...], w_ref[g * GROUP_W:(g + 1) * GROUP_W, :])
    o_ref[...] = acc


def _outproj(x2d, ys, w_out, layer, *, tm=512):
    n_tok = x2d.shape[0]
    y_spec = pl.BlockSpec((tm, GROUP_W), lambda i: (i, 0))
    return pl.pallas_call(
        _outproj_kernel,
        out_shape=jax.ShapeDtypeStruct((n_tok, D_MODEL), F32),
        grid=(n_tok // tm,),
        in_specs=[pl.BlockSpec((tm, D_MODEL), lambda i: (i, 0)), y_spec, y_spec, y_spec, y_spec,
                  pl.BlockSpec((None, D_MODEL, D_MODEL), lambda i: (layer, 0, 0))],
        out_specs=pl.BlockSpec((tm, D_MODEL), lambda i: (i, 0)),
        compiler_params=pltpu.CompilerParams(
            dimension_semantics=("parallel",), vmem_limit_bytes=VMEM_LIMIT),
        name="outproj",
    )(x2d, *ys, w_out)


def _ffn_kernel(x_ref, nw_ref, wg_ref, wu_ref, wd_ref, fw_ref, o_ref, n_scr, *, n_f, final):
    f = pl.program_id(1)

    @pl.when(f == 0)
    def _():
        x = x_ref[...]
        n_scr[...] = _rms(x, nw_ref[...]).astype(BF16)
        o_ref[...] = x

    n = n_scr[...]
    gate = jnp.dot(n, wg_ref[...], preferred_element_type=F32)
    up = jnp.dot(n, wu_ref[...], preferred_element_type=F32)
    hid = (_silu(gate) * up).astype(BF16)
    o_ref[...] += jnp.dot(hid, wd_ref[...], preferred_element_type=F32)

    if final:
        @pl.when(f == n_f - 1)
        def _():
            o_ref[...] = _rms(o_ref[...], fw_ref[...])


def _ffn(x2d, norm_w, w_gate, w_up, w_down, final_w, layer, *, final, tm=1024, tf=512):
    n_tok = x2d.shape[0]
    n_f = FFN_HIDDEN // tf
    return pl.pallas_call(
        functools.partial(_ffn_kernel, n_f=n_f, final=final),
        out_shape=jax.ShapeDtypeStruct((n_tok, D_MODEL), F32),
        grid=(n_tok // tm, n_f),
        in_specs=[
            pl.BlockSpec((tm, D_MODEL), lambda i, f: (i, 0)),
            pl.BlockSpec((1, D_MODEL), lambda i, f: (0, 0)),
            pl.BlockSpec((None, D_MODEL, tf), lambda i, f: (layer, 0, f)),
            pl.BlockSpec((None, D_MODEL, tf), lambda i, f: (layer, 0, f)),
            pl.BlockSpec((None, tf, D_MODEL), lambda i, f: (layer, f, 0)),
            pl.BlockSpec((1, D_MODEL), lambda i, f: (0, 0)),
        ],
        out_specs=pl.BlockSpec((tm, D_MODEL), lambda i, f: (i, 0)),
        scratch_shapes=[pltpu.VMEM((tm, D_MODEL), BF16)],
        compiler_params=pltpu.CompilerParams(
            dimension_semantics=("parallel", "arbitrary"), vmem_limit_bytes=VMEM_LIMIT),
        name="ffn_final" if final else "ffn",
    )(x2d, norm_w.reshape(1, D_MODEL), w_gate, w_up, w_down, final_w.reshape(1, D_MODEL))


def _pad_lanes(v, offset, width):
    v = v.astype(F32)
    return jnp.concatenate([jnp.zeros((offset,), F32), v, jnp.zeros((width - offset - v.shape[0],), F32)])


def _rows(rows, width):
    rows = [r.astype(F32).reshape(width) for r in rows]
    return jnp.stack(rows + [jnp.zeros((width,), F32)] * (8 - len(rows)))


def _perm_kernel(w_ref, o_ref):
    o_ref[:, 0:2048] = w_ref[:, 0:2048].astype(BF16)
    o_ref[:, 2048:4096] = w_ref[:, 2056:4104].astype(BF16)
    o_ref[:, 4096:4096 + RWKV_USED] = w_ref[:, 5136:6928].astype(BF16)
    lane = lax.broadcasted_iota(jnp.int32, (w_ref.shape[0], LANES), 1)
    gdn_ab = w_ref[:, 2048:2176]
    ml_if = w_ref[:, 4096:4224]
    gates = jnp.where(lane < 8, gdn_ab, jnp.where(lane < 16, ml_if, 0.0))
    o_ref[:, GATES_BLK * LANES:(GATES_BLK + 1) * LANES] = gates.astype(BF16)
    o_ref[:, (GATES_BLK + 1) * LANES:(GATES_BLK + 2) * LANES] = jnp.zeros((w_ref.shape[0], LANES), BF16)
    o_ref[:, RGLRU_BLK * 1024:U_COLS] = w_ref[:, 4112:5136].astype(BF16)


def _perm_w_in(w, *, tr=256):
    depth, d, n_in = w.shape
    return pl.pallas_call(
        _perm_kernel,
        out_shape=jax.ShapeDtypeStruct((depth, d, U_COLS), BF16),
        grid=(depth, d // tr),
        in_specs=[pl.BlockSpec((None, tr, n_in), lambda l, i: (l, i, 0))],
        out_specs=pl.BlockSpec((None, tr, U_COLS), lambda l, i: (l, i, 0)),
        compiler_params=pltpu.CompilerParams(
            dimension_semantics=("parallel", "parallel"), vmem_limit_bytes=VMEM_LIMIT),
        name="perm_w_in",
    )(w)


def kernel(x, attn_norm, w_in, gdn_conv, gdn_a_log, gdn_dt_bias, gdn_norm, mlstm_conv, mlstm_b_i, mlstm_b_f, mlstm_norm, rglru_conv, rglru_conv_b, rglru_w_a, rglru_b_a, rglru_w_x, rglru_b_x, rglru_lambda, rwkv_mu, rwkv_w0, rwkv_w2, rwkv_a0, rwkv_a2, rwkv_g2, rwkv_k_k, rwkv_k_a, rwkv_r_k, rwkv_ln_w, rwkv_ln_b, w_out, ffn_norm, ffn_w_gate, ffn_w_up, ffn_w_down, final_norm):
    bsz, seq, d = x.shape
    depth = w_in.shape[0]
    x2d = x.reshape(bsz * seq, d)
    w_perm = _perm_w_in(w_in)
    w_out_bf, w_gate_bf, w_up_bf, w_down_bf = [w.astype(BF16) for w in (w_out, ffn_w_gate, ffn_w_up, ffn_w_down)]
    for l in range(depth):
        u = _inproj(x2d, attn_norm[l], w_perm, l).reshape(bsz, seq, U_COLS)

        gdn_prm = _rows([_pad_lanes(gdn_a_log[l], 0, LANES), _pad_lanes(gdn_dt_bias[l], 0, LANES),
                         gdn_norm[l]], LANES)
        y_gdn = _gdn(u, gdn_conv[l], gdn_prm)

        ml_bias = _pad_lanes(mlstm_b_i[l], 2 * N_HEADS, LANES) + _pad_lanes(mlstm_b_f[l], 3 * N_HEADS, LANES)
        ml_prm = _rows([ml_bias] + [mlstm_norm[l, h] for h in range(N_HEADS)], LANES)
        y_ml = _mlstm(u, mlstm_conv[l], ml_prm)

        rg_prm = _rows([rglru_conv_b[l], rglru_b_a[l], rglru_b_x[l], rglru_lambda[l]], GROUP_W)
        y_rg = _rglru(u, rglru_conv[l], rg_prm, rglru_w_a[l].astype(BF16), rglru_w_x[l].astype(BF16))

        rw_prm = _rows([rwkv_w0[l], rwkv_a0[l], rwkv_k_k[l], rwkv_k_a[l], rwkv_r_k[l],
                        rwkv_ln_w[l], rwkv_ln_b[l]], GROUP_W)
        zeros_lora = jnp.zeros((RWKV_N, GROUP_W), F32)
        w_lora = jnp.concatenate([
            jnp.concatenate([rwkv_w2[l], zeros_lora], axis=1),
            jnp.concatenate([zeros_lora, rwkv_a2[l]], axis=1)], axis=0).astype(BF16)
        y_rw = _rwkv(u, rwkv_mu[l].reshape(1, RWKV_USED), rw_prm, w_lora, rwkv_g2[l].astype(BF16))

        ys = [y.reshape(bsz * seq, GROUP_W) for y in (y_gdn, y_ml, y_rg, y_rw)]
        x2d = _outproj(x2d, ys, w_out_bf, l)
        x2d = _ffn(x2d, ffn_norm[l], w_gate_bf, w_up_bf, w_down_bf, final_norm, l, final=(l == depth - 1))
    return x2d.reshape(bsz, seq, d)
```

```python
import functools
import itertools

import jax
import jax.numpy as jnp
from jax import lax
from jax.experimental import pallas as pl
from jax.experimental.pallas import tpu as pltpu

F32 = jnp.float32
BF16 = jnp.bfloat16

D_MODEL = 2048
GROUP_W = 512
HEAD_D = 128
N_HEADS = 4
RWKV_N = 64
N_PAIRS = GROUP_W // (2 * RWKV_N)
CONV_K = 4
CHUNK = 64
FFN_HIDDEN = 5632
NORM_EPS = 1e-6
RWKV_LN_EPS = 64e-5
RGLRU_C = 8.0
LANES = 128
HALO = 8
INV_LEVELS = 5

U_COLS = 7168
GDN_C0 = 0
MLSTM_C0 = 2048
RWKV_C0 = 4096
RWKV_USED = 1792
GATES_C0 = 5888
RGLRU_C0 = 6144
OUT_GDN, OUT_MLSTM, OUT_RGLRU, OUT_RWKV = 0, GROUP_W, 2 * GROUP_W, 3 * GROUP_W

VMEM_LIMIT = 60 * 1024 * 1024

_NN = (((1,), (0,)), ((), ()))
_NT = (((1,), (1,)), ((), ()))
_TN = (((0,), (0,)), ((), ()))


def _dot(a, b, dims=_NN):
    return lax.dot_general(a, b, dims, preferred_element_type=F32)


def _mm(a, b, dims=_NN):
    return _dot(a.astype(BF16), b.astype(BF16), dims)


def _sigmoid(x):
    return 1.0 / (1.0 + jnp.exp(-x))


def _softplus(x):
    return jnp.maximum(x, 0.0) + jnp.log1p(jnp.exp(-jnp.abs(x)))


def _silu(x):
    return x * _sigmoid(x)


def _rms(x, w, eps=NORM_EPS):
    return x * lax.rsqrt(jnp.mean(x * x, axis=-1, keepdims=True) + eps) * w


def _chunk_cumsum(x):
    row = lax.broadcasted_iota(jnp.int32, x.shape, 0) % CHUNK
    d = 1
    while d < CHUNK:
        x = x + jnp.where(row >= d, pltpu.roll(x, d, 0), 0.0)
        d *= 2
    return x


def _split(a):
    hi = a.astype(BF16)
    return hi, (a - hi.astype(F32)).astype(BF16)


def _mm3(a, b):
    return _dot(a[0], b[0]) + (_dot(a[0], b[1]) + _dot(a[1], b[0]))


def _inv_unit_lower_steps(a_list, out):
    n = a_list[0].shape[0]
    eye = (lax.broadcasted_iota(jnp.int32, (n, n), 0) == lax.broadcasted_iota(jnp.int32, (n, n), 1)).astype(F32)
    inv = [eye - a for a in a_list]
    p = [(-a).astype(BF16) for a in a_list]
    for _ in range(INV_LEVELS):
        p = [_dot(x, x).astype(BF16) for x in p]
        yield
        inv = [x + _dot(x.astype(BF16), y) for x, y in zip(inv, p)]
        yield
    a_s = [_split(a) for a in a_list]
    t_s = [_split(t) for t in inv]
    a_t = [_mm3(x, y) for x, y in zip(a_s, t_s)]
    yield
    out[:] = [t + _dot(ts[0], (eye - t - x).astype(BF16)) for t, ts, x in zip(inv, t_s, a_t)]


def _causal_conv(buf, x, conv_w):
    tt = x.shape[0]
    buf[HALO:HALO + tt, :] = x
    xx = buf[...]
    acc = conv_w[0:1, :] * xx
    for j in range(1, CONV_K):
        acc = conv_w[j:j + 1, :] * xx + pltpu.roll(acc, 1, 0)
    buf[0:HALO, :] = x[tt - HALO:tt, :]
    return acc[HALO:, :]


def _brows(b):
    return slice(b * CHUNK, (b + 1) * CHUNK)


def _hcols(h, base=0):
    return slice(base + h * HEAD_D, base + (h + 1) * HEAD_D)


def _inproj_kernel(x_ref, nw_ref, w_ref, o_ref, n_scr):
    @pl.when(pl.program_id(1) == 0)
    def _():
        n_scr[...] = _rms(x_ref[...], nw_ref[...]).astype(BF16)

    o_ref[...] = jnp.dot(n_scr[...], w_ref[...], preferred_element_type=F32)


def _inproj(x2d, norm_w, w_perm, layer, *, tm=1024, tn=1024):
    n_tok = x2d.shape[0]
    return pl.pallas_call(
        _inproj_kernel,
        out_shape=jax.ShapeDtypeStruct((n_tok, U_COLS), F32),
        grid=(n_tok // tm, U_COLS // tn),
        in_specs=[
            pl.BlockSpec((tm, D_MODEL), lambda i, j: (i, 0)),
            pl.BlockSpec((1, D_MODEL), lambda i, j: (0, 0)),
            pl.BlockSpec((None, D_MODEL, tn), lambda i, j: (layer, 0, j)),
        ],
        out_specs=pl.BlockSpec((tm, tn), lambda i, j: (i, j)),
        scratch_shapes=[pltpu.VMEM((tm, D_MODEL), BF16)],
        compiler_params=pltpu.CompilerParams(
            dimension_semantics=("parallel", "arbitrary"), vmem_limit_bytes=VMEM_LIMIT),
        name="inproj",
    )(x2d, norm_w.reshape(1, D_MODEL), w_perm)


def _gdn_steps(u_ref, cw_ref, prm_ref, o_ref, buf, s_ref):
    nb = u_ref.shape[0]
    cw = cw_ref[...]
    qkv = _silu(jnp.concatenate(
        [_causal_conv(buf.at[b], u_ref[b, :, GDN_C0:GDN_C0 + 3 * GROUP_W], cw) for b in range(nb)], axis=0))
    yield
    gates = u_ref[:, :, GATES_C0:GATES_C0 + LANES].reshape(nb * CHUNK, LANES)
    prm = prm_ref[...]
    g_all = -jnp.exp(prm[0:1, :]) * _softplus(gates + prm[1:2, :])
    beta_all = _sigmoid(gates)
    gc_all = _chunk_cumsum(g_all)
    norm_w = prm[2:3, :]
    gc_t = [gc_all[_brows(b), :].T for b in range(nb)]
    yield

    qn, kn = [], []
    for h in range(N_HEADS):
        q = qkv[:, _hcols(h)]
        k = qkv[:, _hcols(h, GROUP_W)]
        qn.append(q * (lax.rsqrt(jnp.sum(q * q, axis=-1, keepdims=True) + 1e-6) * (HEAD_D ** -0.5)))
        kn.append(k * lax.rsqrt(jnp.sum(k * k, axis=-1, keepdims=True) + 1e-6))
    yield

    ri = lax.broadcasted_iota(jnp.int32, (CHUNK, CHUNK), 0)
    ci = lax.broadcasted_iota(jnp.int32, (CHUNK, CHUNK), 1)
    causal = ri >= ci
    strict = ri > ci

    probs = [(b, h) for b in range(nb) for h in range(N_HEADS)]
    q = [qn[h][_brows(b), :] for b, h in probs]
    k = [kn[h][_brows(b), :] for b, h in probs]
    v = [qkv[_brows(b), _hcols(h, 2 * GROUP_W)] for b, h in probs]
    beta = [beta_all[_brows(b), N_HEADS + h:N_HEADS + h + 1] for b, h in probs]
    gcol = [gc_all[_brows(b), h:h + 1] for b, h in probs]
    grow = [gc_t[b][h:h + 1, :] for b, h in probs]
    decay = [jnp.where(causal, jnp.exp(jnp.where(causal, gc - gr, 0.0)), 0.0) for gc, gr in zip(gcol, grow)]
    yield
    kb = [x * y for x, y in zip(k, beta)]
    raw = [_mm(jnp.concatenate([x, y], axis=0), z, _NT) for x, y, z in zip(kb, q, k)]
    yield
    a_mat = [jnp.where(strict, r[:CHUNK] * d, 0.0) for r, d in zip(raw, decay)]
    attn = [r[CHUNK:] * d for r, d in zip(raw, decay)]
    t_mat = []
    yield from _inv_unit_lower_steps(a_mat, t_mat)
    eg = [jnp.exp(x) for x in gcol]
    sol = [_mm(t, jnp.concatenate([x * y, z * e], axis=1))
           for t, x, y, z, e in zip(t_mat, v, beta, kb, eg)]
    yield
    s_old = [s_ref[b, h] for b, h in probs]
    ws = [_mm(jnp.concatenate([x[:, HEAD_D:], y * e], axis=0), s)
          for x, y, e, s in zip(sol, q, eg, s_old)]
    yield
    v_new = [x[:, :HEAD_D] - y[:CHUNK] for x, y in zip(sol, ws)]
    o = [y[CHUNK:] + _mm(a, x) for y, a, x in zip(ws, attn, v_new)]
    yield
    glast = [x[CHUNK - 1:CHUNK, :] for x in gcol]
    s_new = [s * jnp.exp(gl) + _mm(x * jnp.exp(gl - gc), vn, _TN)
             for s, gl, x, gc, vn in zip(s_old, glast, k, gcol, v_new)]
    for (b, h), s in zip(probs, s_new):
        s_ref[b, h] = s
    yield
    for (b, h), x in zip(probs, o):
        z = u_ref[b, :, _hcols(h, GDN_C0 + 3 * GROUP_W)]
        o_ref[b, :, _hcols(h, OUT_GDN)] = (_rms(x, norm_w) * _silu(z)).astype(BF16)


def _mlstm_steps(u_ref, cw_ref, prm_ref, o_ref, buf, c_ref, n_ref, m_ref):
    nb = u_ref.shape[0]
    cw = cw_ref[...]
    qk = _silu(jnp.concatenate(
        [_causal_conv(buf.at[b], u_ref[b, :, MLSTM_C0:MLSTM_C0 + 2 * GROUP_W], cw) for b in range(nb)], axis=0))
    yield
    gates = u_ref[:, :, GATES_C0:GATES_C0 + LANES].reshape(nb * CHUNK, LANES)
    prm = prm_ref[...]
    pre = gates + prm[0:1, :]
    b_all = _chunk_cumsum(-_softplus(-pre))
    b_tr = [b_all[_brows(b), :].T for b in range(nb)]
    i_tr = [pre[_brows(b), :].T for b in range(nb)]
    yield

    ri = lax.broadcasted_iota(jnp.int32, (CHUNK, CHUNK), 0)
    ci = lax.broadcasted_iota(jnp.int32, (CHUNK, CHUNK), 1)
    causal = ri >= ci

    probs = [(b, h) for b in range(nb) for h in range(N_HEADS)]
    li = [2 * N_HEADS + h for _, h in probs]
    lf = [3 * N_HEADS + h for _, h in probs]
    q = [qk[_brows(b), _hcols(h)] for b, h in probs]
    k = [qk[_brows(b), _hcols(h, GROUP_W)] * (HEAD_D ** -0.5) for b, h in probs]
    v = [u_ref[b, :, _hcols(h, MLSTM_C0 + 2 * GROUP_W)] for b, h in probs]
    bcol = [b_all[_brows(b), l:l + 1] for (b, _), l in zip(probs, lf)]
    icol = [pre[_brows(b), l:l + 1] for (b, _), l in zip(probs, li)]
    brow = [b_tr[b][l:l + 1, :] for (b, _), l in zip(probs, lf)]
    irow = [i_tr[b][l:l + 1, :] for (b, _), l in zip(probs, li)]
    m_old = [m_ref[b, h][0:1, 0:1] for b, h in probs]
    c_old = [c_ref[b, h] for b, h in probs]
    n_old = [n_ref[b, h][0:1, :] for b, h in probs]
    yield

    d = [jnp.where(causal, bc - br + ir, -jnp.inf) for bc, br, ir in zip(bcol, brow, irow)]
    inter = [bc + m for bc, m in zip(bcol, m_old)]
    m_t = [jnp.maximum(x, jnp.max(y, axis=-1, keepdims=True)) for x, y in zip(inter, d)]
    yield
    qk_raw = [_mm(x, y, _NT) for x, y in zip(q, k)]
    qc = [_mm(x, c) for x, c in zip(q, c_old)]
    yield
    s = [r * jnp.exp(x - m) for r, x, m in zip(qk_raw, d, m_t)]
    a = [jnp.exp(x - m) for x, m in zip(inter, m_t)]
    yield
    sv = [_mm(x, y) for x, y in zip(s, v)]
    yield
    num = [x * y + z for x, y, z in zip(a, qc, sv)]
    den = [x * jnp.sum(y * n, axis=-1, keepdims=True) + jnp.sum(z, axis=-1, keepdims=True)
           for x, y, n, z in zip(a, q, n_old, s)]
    h_t = [x / jnp.maximum(jnp.abs(y), jnp.exp(-m)) for x, y, m in zip(num, den, m_t)]
    yield
    g = [x[CHUNK - 1:CHUNK, :] for x in bcol]
    w_log = [x - bc + ic for x, bc, ic in zip(g, bcol, icol)]
    m_new = [jnp.maximum(x + m, jnp.max(w, axis=0, keepdims=True)) for x, m, w in zip(g, m_old, w_log)]
    scale = [jnp.exp(x + m - mn) for x, m, mn in zip(g, m_old, m_new)]
    kw = [x * jnp.exp(w - mn) for x, w, mn in zip(k, w_log, m_new)]
    yield
    c_new = [sc * c + _mm(x, y, _TN) for sc, c, x, y in zip(scale, c_old, kw, v)]
    n_new = [sc * n + jnp.sum(x, axis=0, keepdims=True) for sc, n, x in zip(scale, n_old, kw)]
    for (b, h), c, n, m in zip(probs, c_new, n_new, m_new):
        c_ref[b, h] = c
        n_ref[b, h] = jnp.broadcast_to(n, (8, HEAD_D))
        m_ref[b, h] = jnp.broadcast_to(m, (8, HEAD_D))
    yield
    for (b, h), x in zip(probs, h_t):
        o_gate = u_ref[b, :, _hcols(h, MLSTM_C0 + 3 * GROUP_W)]
        o_ref[b, :, _hcols(h, OUT_MLSTM)] = _rms(_sigmoid(o_gate) * x, prm[1 + h:2 + h, :]).astype(BF16)


def _rglru_steps(u_ref, cw_ref, prm_ref, wa_ref, wx_ref, o_ref, buf, h_ref):
    nb = u_ref.shape[0]
    prm = prm_ref[...]
    cw = cw_ref[...]
    xb = jnp.concatenate(
        [_causal_conv(buf.at[b], u_ref[b, :, RGLRU_C0:RGLRU_C0 + GROUP_W], cw) for b in range(nb)], axis=0)
    xb = xb + prm[0:1, :]
    yield
    ra = []
    rx = []
    for n in range(N_HEADS):
        blk = xb[:, _hcols(n)].astype(BF16)
        ra.append(_dot(blk, wa_ref[n]))
        rx.append(_dot(blk, wx_ref[n]))
    yield
    r = _sigmoid(jnp.concatenate(ra, axis=1) + prm[1:2, :])
    i = _sigmoid(jnp.concatenate(rx, axis=1) + prm[2:3, :])
    log_a = -RGLRU_C * r * _softplus(-prm[3:4, :])
    a = jnp.exp(log_a)
    th = jnp.tanh(log_a)
    bb = jnp.sqrt(-2.0 * th / (1.0 - th)) * (i * xb)
    yield
    row = lax.broadcasted_iota(jnp.int32, a.shape, 0) % HALO
    d = 1
    while d < HALO:
        keep = row >= d
        bb = jnp.where(keep, a * pltpu.roll(bb, d, 0) + bb, bb)
        a = jnp.where(keep, a * pltpu.roll(a, d, 0), a)
        d *= 2
        yield
    carry = [h_ref[b][0:1, :] for b in range(nb)]
    groups = [[] for _ in range(nb)]
    for g in range(CHUNK // HALO):
        for b in range(nb):
            lo = b * CHUNK + g * HALO
            hg = bb[lo:lo + HALO, :] + a[lo:lo + HALO, :] * carry[b]
            groups[b].append(hg)
            carry[b] = hg[HALO - 1:HALO, :]
    yield
    for b in range(nb):
        h_ref[b] = jnp.broadcast_to(carry[b], (8, GROUP_W))
        gate = u_ref[b, :, RGLRU_C0 + GROUP_W:RGLRU_C0 + 2 * GROUP_W]
        gelu = 0.5 * gate * (1.0 + jnp.tanh(0.7978845608028654 * (gate + 0.044715 * gate * gate * gate)))
        o_ref[b, :, OUT_RGLRU:OUT_RGLRU + GROUP_W] = (jnp.concatenate(groups[b], axis=0) * gelu).astype(BF16)


def _pair_sum(x, lo_mask):
    s_lo = jnp.sum(jnp.where(lo_mask, x, 0.0), axis=-1, keepdims=True)
    s_hi = jnp.sum(jnp.where(lo_mask, 0.0, x), axis=-1, keepdims=True)
    return jnp.where(lo_mask, s_lo, s_hi)


def _stack_heads(x, lo_mask):
    return jnp.concatenate([jnp.where(lo_mask, x, 0.0), jnp.where(lo_mask, 0.0, x)], axis=0)


def _rwkv_steps(u_ref, mu_ref, prm_ref, wlo_ref, g2_ref, o_ref, buf, s_ref):
    nb = u_ref.shape[0]
    mu = mu_ref[...]
    xs = []
    for b in range(nb):
        x = u_ref[b, :, RWKV_C0:RWKV_C0 + RWKV_USED]
        buf[b, HALO:HALO + CHUNK, :] = x
        x_prev = buf[b, HALO - 1:HALO - 1 + CHUNK, :]
        buf[b, 0:HALO, :] = x[CHUNK - HALO:CHUNK, :]
        xs.append(x + (x_prev - x) * mu)
    xs = jnp.concatenate(xs, axis=0)
    yield
    prm = prm_ref[...]
    r_all = xs[:, 0:GROUP_W]
    k_all = xs[:, GROUP_W:2 * GROUP_W]
    v_all = xs[:, 2 * GROUP_W:3 * GROUP_W]
    lora = xs[:, 3 * GROUP_W:3 * GROUP_W + LANES]
    lane = lax.broadcasted_iota(jnp.int32, lora.shape, 1)
    lora = jnp.where(lane < RWKV_N, jnp.tanh(lora), lora)
    wa = _mm(lora, wlo_ref[...])
    g_all = _mm(_sigmoid(xs[:, 3 * GROUP_W + LANES:3 * GROUP_W + 2 * LANES]), g2_ref[...])
    yield
    w_raw = -_softplus(-(prm[0:1, :] + wa[:, :GROUP_W])) - 0.5
    lw_all = -jnp.exp(w_raw)
    a_all = _sigmoid(prm[1:2, :] + wa[:, GROUP_W:])
    kk_all = k_all * prm[2:3, :]
    k2_all = k_all * (1.0 + (a_all - 1.0) * prm[3:4, :])
    yield
    lc_all = _chunk_cumsum(lw_all)
    yield

    lo_mask = lax.broadcasted_iota(jnp.int32, (CHUNK, LANES), 1) < RWKV_N
    n2 = 2 * CHUNK
    ri = lax.broadcasted_iota(jnp.int32, (n2, n2), 0)
    ci = lax.broadcasted_iota(jnp.int32, (n2, n2), 1)
    same = (ri // CHUNK) == (ci // CHUNK)
    strict = same & ((ci % CHUNK) < (ri % CHUNK))
    incl = same & ((ci % CHUNK) <= (ri % CHUNK))

    probs = [(b, p) for b in range(nb) for p in range(N_PAIRS)]

    def sel(arr):
        return [arr[_brows(b), p * LANES:(p + 1) * LANES] for b, p in probs]

    r, v, k2, a_sig, lw, lc = sel(r_all), sel(v_all), sel(k2_all), sel(a_all), sel(lw_all), sel(lc_all)
    kk = [x * lax.rsqrt(_pair_sum(x * x, lo_mask) + 1e-6) for x in sel(kk_all)]
    yield
    w_incl = [jnp.exp(x) for x in lc]
    w_inv = [jnp.exp(-x) for x in lc]
    a_hat = [_stack_heads(-x * jnp.exp(c - w), lo_mask) for x, c, w in zip(kk, lc, lw)]
    b_hat = [_stack_heads(x * y * w, lo_mask) for x, y, w in zip(kk, a_sig, w_inv)]
    yield
    k_hat = [_stack_heads(x * w, lo_mask) for x, w in zip(k2, w_inv)]
    r_hat = [_stack_heads(x * w, lo_mask) for x, w in zip(r, w_incl)]
    v_st = [_stack_heads(x, lo_mask) for x in v]
    s_old = [s_ref[b, p] for b, p in probs]
    yield

    ar = [jnp.concatenate([x, y], axis=0).astype(BF16) for x, y in zip(a_hat, r_hat)]
    bk = [jnp.concatenate([x, y], axis=0).astype(BF16) for x, y in zip(b_hat, k_hat)]
    cross = [_dot(x, y, _NT) for x, y in zip(ar, bk)]
    yield
    a_ab = [jnp.where(strict, x[:n2, :n2], 0.0) for x in cross]
    a_ak = [jnp.where(strict, x[:n2, n2:], 0.0) for x in cross]
    m_r = [jnp.concatenate([jnp.where(incl, x[n2:, :n2], 0.0), jnp.where(incl, x[n2:, n2:], 0.0)], axis=1)
           for x in cross]
    yield
    ars = [_dot(x, s.astype(BF16), _NT) for x, s in zip(ar, s_old)]
    akv = [_mm(y, z) for y, z in zip(a_ak, v_st)]
    t_mat = []
    yield from _inv_unit_lower_steps([-x for x in a_ab], t_mat)
    pv = [_mm(t, x[:n2] + y) for t, x, y in zip(t_mat, ars, akv)]
    yield
    pvv = [jnp.concatenate([x, y], axis=0).astype(BF16) for x, y in zip(pv, v_st)]
    y_st = [x[n2:] + _dot(m.astype(BF16), z) for x, m, z in zip(ars, m_r, pvv)]
    yield
    s_new = [(s + _dot(x, y, _TN)) * w[CHUNK - 1:CHUNK, :] for s, x, y, w in zip(s_old, pvv, bk, w_incl)]
    for (b, p), s in zip(probs, s_new):
        s_ref[b, p] = s
    yield
    for (b, p), ys, rr, kk2, vv in zip(probs, y_st, r, k2, v):
        cols = slice(p * LANES, (p + 1) * LANES)
        y = ys[:CHUNK, :] + ys[CHUNK:, :]
        yc = y - _pair_sum(y, lo_mask) * (1.0 / RWKV_N)
        var = _pair_sum(yc * yc, lo_mask) * (1.0 / RWKV_N)
        y = yc * lax.rsqrt(var + RWKV_LN_EPS) * prm[5:6, cols] + prm[6:7, cols]
        y = y + _pair_sum(rr * kk2 * prm[4:5, cols], lo_mask) * vv
        o_ref[b, :, OUT_RWKV + p * LANES:OUT_RWKV + (p + 1) * LANES] = (y * g_all[_brows(b), cols]).astype(BF16)


def _mixers_kernel(u_ref, gdn_cw, gdn_prm, ml_cw, ml_prm, rg_cw, rg_prm, rg_wa, rg_wx, rw_mu, rw_prm, rw_wlo, rw_g2,
                   o_ref, gdn_buf, gdn_s, ml_buf, ml_c, ml_n, ml_m, rg_buf, rg_h, rw_buf, rw_s):
    @pl.when(pl.program_id(0) == 0)
    def _():
        for buf in (gdn_buf, ml_buf, rg_buf, rw_buf):
            buf[:, 0:HALO, :] = jnp.zeros((buf.shape[0], HALO, buf.shape[2]), F32)
        for state in (gdn_s, ml_c, ml_n, ml_m, rg_h, rw_s):
            state[...] = jnp.zeros_like(state)

    stages = [
        _rwkv_steps(u_ref, rw_mu, rw_prm, rw_wlo, rw_g2, o_ref, rw_buf, rw_s),
        _gdn_steps(u_ref, gdn_cw, gdn_prm, o_ref, gdn_buf, gdn_s),
        _mlstm_steps(u_ref, ml_cw, ml_prm, o_ref, ml_buf, ml_c, ml_n, ml_m),
        _rglru_steps(u_ref, rg_cw, rg_prm, rg_wa, rg_wx, o_ref, rg_buf, rg_h),
    ]
    for _ in itertools.zip_longest(*stages):
        pass


def _mixers(u, gdn_cw, gdn_prm, ml_cw, ml_prm, rg_cw, rg_prm, rg_wa, rg_wx, rw_mu, rw_prm, rw_wlo, rw_g2):
    bsz, seq, _ = u.shape

    def whole(a):
        return pl.BlockSpec(a.shape, lambda t, _n=a.ndim: (0,) * _n)

    params = (gdn_cw, gdn_prm, ml_cw, ml_prm, rg_cw, rg_prm, rg_wa, rg_wx, rw_mu, rw_prm, rw_wlo, rw_g2)
    return pl.pallas_call(
        _mixers_kernel,
        out_shape=jax.ShapeDtypeStruct((bsz, seq, D_MODEL), BF16),
        grid=(seq // CHUNK,),
        in_specs=[pl.BlockSpec((bsz, CHUNK, U_COLS), lambda t: (0, t, 0))] + [whole(a) for a in params],
        out_specs=pl.BlockSpec((bsz, CHUNK, D_MODEL), lambda t: (0, t, 0)),
        scratch_shapes=[
            pltpu.VMEM((bsz, HALO + CHUNK, 3 * GROUP_W), F32),
            pltpu.VMEM((bsz, N_HEADS, HEAD_D, HEAD_D), F32),
            pltpu.VMEM((bsz, HALO + CHUNK, 2 * GROUP_W), F32),
            pltpu.VMEM((bsz, N_HEADS, HEAD_D, HEAD_D), F32),
            pltpu.VMEM((bsz, N_HEADS, 8, HEAD_D), F32),
            pltpu.VMEM((bsz, N_HEADS, 8, HEAD_D), F32),
            pltpu.VMEM((bsz, HALO + CHUNK, GROUP_W), F32),
            pltpu.VMEM((bsz, 8, GROUP_W), F32),
            pltpu.VMEM((bsz, HALO + CHUNK, RWKV_USED), F32),
            pltpu.VMEM((bsz, N_PAIRS, LANES, LANES), F32),
        ],
        compiler_params=pltpu.CompilerParams(
            dimension_semantics=("arbitrary",), vmem_limit_bytes=VMEM_LIMIT),
        name="mixers",
    )(u, *params)


def _outproj_kernel(x_ref, y_ref, w_ref, o_ref):
    o_ref[...] = x_ref[...] + _dot(y_ref[...], w_ref[...])


def _outproj(x2d, mix, w_out, layer, *, tm=512):
    n_tok = x2d.shape[0]
    return pl.pallas_call(
        _outproj_kernel,
        out_shape=jax.ShapeDtypeStruct((n_tok, D_MODEL), F32),
        grid=(n_tok // tm,),
        in_specs=[pl.BlockSpec((tm, D_MODEL), lambda i: (i, 0)),
                  pl.BlockSpec((tm, D_MODEL), lambda i: (i, 0)),
                  pl.BlockSpec((None, D_MODEL, D_MODEL), lambda i: (layer, 0, 0))],
        out_specs=pl.BlockSpec((tm, D_MODEL), lambda i: (i, 0)),
        compiler_params=pltpu.CompilerParams(
            dimension_semantics=("parallel",), vmem_limit_bytes=VMEM_LIMIT),
        name="outproj",
    )(x2d, mix, w_out)


def _ffn_kernel(x_ref, nw_ref, wg_ref, wu_ref, wd_ref, fw_ref, o_ref, n_scr, *, n_f, final):
    f = pl.program_id(1)

    @pl.when(f == 0)
    def _():
        x = x_ref[...]
        n_scr[...] = _rms(x, nw_ref[...]).astype(BF16)
        o_ref[...] = x

    n = n_scr[...]
    gate = jnp.dot(n, wg_ref[...], preferred_element_type=F32)
    up = jnp.dot(n, wu_ref[...], preferred_element_type=F32)
    hid = (_silu(gate) * up).astype(BF16)
    o_ref[...] += jnp.dot(hid, wd_ref[...], preferred_element_type=F32)

    if final:
        @pl.when(f == n_f - 1)
        def _():
            o_ref[...] = _rms(o_ref[...], fw_ref[...])


def _ffn(x2d, norm_w, w_gate, w_up, w_down, final_w, layer, *, final, tm=1024, tf=512):
    n_tok = x2d.shape[0]
    n_f = FFN_HIDDEN // tf
    return pl.pallas_call(
        functools.partial(_ffn_kernel, n_f=n_f, final=final),
        out_shape=jax.ShapeDtypeStruct((n_tok, D_MODEL), F32),
        grid=(n_tok // tm, n_f),
        in_specs=[
            pl.BlockSpec((tm, D_MODEL), lambda i, f: (i, 0)),
            pl.BlockSpec((1, D_MODEL), lambda i, f: (0, 0)),
            pl.BlockSpec((None, D_MODEL, tf), lambda i, f: (layer, 0, f)),
            pl.BlockSpec((None, D_MODEL, tf), lambda i, f: (layer, 0, f)),
            pl.BlockSpec((None, tf, D_MODEL), lambda i, f: (layer, f, 0)),
            pl.BlockSpec((1, D_MODEL), lambda i, f: (0, 0)),
        ],
        out_specs=pl.BlockSpec((tm, D_MODEL), lambda i, f: (i, 0)),
        scratch_shapes=[pltpu.VMEM((tm, D_MODEL), BF16)],
        compiler_params=pltpu.CompilerParams(
            dimension_semantics=("parallel", "arbitrary"), vmem_limit_bytes=VMEM_LIMIT),
        name="ffn_final" if final else "ffn",
    )(x2d, norm_w.reshape(1, D_MODEL), w_gate, w_up, w_down, final_w.reshape(1, D_MODEL))


def _pad_lanes(v, offset, width):
    v = v.astype(F32)
    return jnp.concatenate([jnp.zeros((offset,), F32), v, jnp.zeros((width - offset - v.shape[0],), F32)])


def _rows(rows, width):
    rows = [r.astype(F32).reshape(width) for r in rows]
    return jnp.stack(rows + [jnp.zeros((width,), F32)] * (8 - len(rows)))


def _perm_kernel(w_ref, o_ref):
    o_ref[:, GDN_C0:GDN_C0 + 2048] = w_ref[:, 0:2048].astype(BF16)
    o_ref[:, MLSTM_C0:MLSTM_C0 + 2048] = w_ref[:, 2056:4104].astype(BF16)
    o_ref[:, RWKV_C0:RWKV_C0 + RWKV_USED] = w_ref[:, 5136:6928].astype(BF16)
    lane = lax.broadcasted_iota(jnp.int32, (w_ref.shape[0], LANES), 1)
    gdn_ab = w_ref[:, 2048:2176]
    ml_if = w_ref[:, 4096:4224]
    gates = jnp.where(lane < 8, gdn_ab, jnp.where(lane < 16, ml_if, 0.0))
    o_ref[:, GATES_C0:GATES_C0 + LANES] = gates.astype(BF16)
    o_ref[:, GATES_C0 + LANES:RGLRU_C0] = jnp.zeros((w_ref.shape[0], RGLRU_C0 - GATES_C0 - LANES), BF16)
    o_ref[:, RGLRU_C0:U_COLS] = w_ref[:, 4112:5136].astype(BF16)


def _perm_w_in(w, *, tr=256):
    depth, d, n_in = w.shape
    return pl.pallas_call(
        _perm_kernel,
        out_shape=jax.ShapeDtypeStruct((depth, d, U_COLS), BF16),
        grid=(depth, d // tr),
        in_specs=[pl.BlockSpec((None, tr, n_in), lambda l, i: (l, i, 0))],
        out_specs=pl.BlockSpec((None, tr, U_COLS), lambda l, i: (l, i, 0)),
        compiler_params=pltpu.CompilerParams(
            dimension_semantics=("parallel", "parallel"), vmem_limit_bytes=VMEM_LIMIT),
        name="perm_w_in",
    )(w)


def kernel(x, attn_norm, w_in, gdn_conv, gdn_a_log, gdn_dt_bias, gdn_norm, mlstm_conv, mlstm_b_i, mlstm_b_f, mlstm_norm, rglru_conv, rglru_conv_b, rglru_w_a, rglru_b_a, rglru_w_x, rglru_b_x, rglru_lambda, rwkv_mu, rwkv_w0, rwkv_w2, rwkv_a0, rwkv_a2, rwkv_g2, rwkv_k_k, rwkv_k_a, rwkv_r_k, rwkv_ln_w, rwkv_ln_b, w_out, ffn_norm, ffn_w_gate, ffn_w_up, ffn_w_down, final_norm):
    bsz, seq, d = x.shape
    depth = w_in.shape[0]
    x2d = x.reshape(bsz * seq, d)
    w_perm = _perm_w_in(w_in)
    w_out_bf, w_gate_bf, w_up_bf, w_down_bf = [w.astype(BF16) for w in (w_out, ffn_w_gate, ffn_w_up, ffn_w_down)]
    for l in range(depth):
        u = _inproj(x2d, attn_norm[l], w_perm, l).reshape(bsz, seq, U_COLS)

        gdn_prm = _rows([_pad_lanes(gdn_a_log[l], 0, LANES), _pad_lanes(gdn_dt_bias[l], 0, LANES),
                         gdn_norm[l]], LANES)
        ml_bias = _pad_lanes(mlstm_b_i[l], 2 * N_HEADS, LANES) + _pad_lanes(mlstm_b_f[l], 3 * N_HEADS, LANES)
        ml_prm = _rows([ml_bias] + [mlstm_norm[l, h] for h in range(N_HEADS)], LANES)
        rg_prm = _rows([rglru_conv_b[l], rglru_b_a[l], rglru_b_x[l], rglru_lambda[l]], GROUP_W)
        rw_prm = _rows([rwkv_w0[l], rwkv_a0[l], rwkv_k_k[l], rwkv_k_a[l], rwkv_r_k[l],
                        rwkv_ln_w[l], rwkv_ln_b[l]], GROUP_W)
        zeros_lora = jnp.zeros((RWKV_N, GROUP_W), F32)
        w_lora = jnp.concatenate([
            jnp.concatenate([rwkv_w2[l], zeros_lora], axis=1),
            jnp.concatenate([zeros_lora, rwkv_a2[l]], axis=1)], axis=0).astype(BF16)
        mix = _mixers(u, gdn_conv[l], gdn_prm, mlstm_conv[l], ml_prm, rglru_conv[l], rg_prm,
                      rglru_w_a[l].astype(BF16), rglru_w_x[l].astype(BF16),
                      rwkv_mu[l].reshape(1, RWKV_USED), rw_prm, w_lora, rwkv_g2[l].astype(BF16))

        x2d = _outproj(x2d, mix.reshape(bsz * seq, d), w_out_bf, l)
        x2d = _ffn(x2d, ffn_norm[l], w_gate_bf, w_up_bf, w_down_bf, final_norm, l, final=(l == depth - 1))
    return x2d.reshape(bsz, seq, d)
```

```python
import functools
import itertools

import jax
import jax.numpy as jnp
from jax import lax
from jax.experimental import pallas as pl
from jax.experimental.pallas import tpu as pltpu

F32 = jnp.float32
BF16 = jnp.bfloat16

D_MODEL = 2048
GROUP_W = 512
HEAD_D = 128
N_HEADS = 4
RWKV_N = 64
N_PAIRS = GROUP_W // (2 * RWKV_N)
CONV_K = 4
CHUNK = 64
FFN_HIDDEN = 5632
NORM_EPS = 1e-6
RWKV_LN_EPS = 64e-5
RGLRU_C = 8.0
LANES = 128
HALO = 8
INV_LEVELS = 5

U_COLS = 7168
GDN_C0 = 0
MLSTM_C0 = 2048
RWKV_C0 = 4096
RWKV_USED = 1792
GATES_C0 = 5888
RGLRU_C0 = 6144
OUT_GDN, OUT_MLSTM, OUT_RGLRU, OUT_RWKV = 0, GROUP_W, 2 * GROUP_W, 3 * GROUP_W

VMEM_LIMIT = 60 * 1024 * 1024

_NN = (((1,), (0,)), ((), ()))
_NT = (((1,), (1,)), ((), ()))
_TN = (((0,), (0,)), ((), ()))


def _dot(a, b, dims=_NN):
    return lax.dot_general(a, b, dims, preferred_element_type=F32)


def _mm(a, b, dims=_NN):
    return _dot(a.astype(BF16), b.astype(BF16), dims)


def _sigmoid(x):
    return 1.0 / (1.0 + jnp.exp(-x))


def _softplus(x):
    return jnp.maximum(x, 0.0) + jnp.log(1.0 + jnp.exp(-jnp.abs(x)))


def _silu(x):
    return x * _sigmoid(x)


def _rms(x, w, eps=NORM_EPS):
    return x * lax.rsqrt(jnp.mean(x * x, axis=-1, keepdims=True) + eps) * w


def _chunk_cumsum(x):
    row = lax.broadcasted_iota(jnp.int32, x.shape, 0) % CHUNK
    d = 1
    while d < CHUNK:
        x = x + jnp.where(row >= d, pltpu.roll(x, d, 0), 0.0)
        d *= 2
    return x


def _chunk_cumsum_mxu(x):
    ri = lax.broadcasted_iota(jnp.int32, (CHUNK, CHUNK), 0)
    ci = lax.broadcasted_iota(jnp.int32, (CHUNK, CHUNK), 1)
    tri = (ri >= ci).astype(BF16)
    hi = x.astype(BF16)
    r1 = x - hi.astype(F32)
    mid = r1.astype(BF16)
    lo = (r1 - mid.astype(F32)).astype(BF16)
    outs = []
    for c in range(x.shape[0] // CHUNK):
        rows = _brows(c)
        outs.append(_dot(tri, hi[rows, :]) + (_dot(tri, mid[rows, :]) + _dot(tri, lo[rows, :])))
    return jnp.concatenate(outs, axis=0)


def _split(a):
    hi = a.astype(BF16)
    return hi, (a - hi.astype(F32)).astype(BF16)


def _mm3(a, b):
    return _dot(a[0], b[0]) + (_dot(a[0], b[1]) + _dot(a[1], b[0]))


def _inv_unit_lower_steps(a_list, out):
    n = a_list[0].shape[0]
    eye = (lax.broadcasted_iota(jnp.int32, (n, n), 0) == lax.broadcasted_iota(jnp.int32, (n, n), 1)).astype(F32)
    inv = [eye - a for a in a_list]
    p = [(-a).astype(BF16) for a in a_list]
    for _ in range(INV_LEVELS):
        p = [_dot(x, x).astype(BF16) for x in p]
        yield
        inv = [x + _dot(x.astype(BF16), y) for x, y in zip(inv, p)]
        yield
    a_s = [_split(a) for a in a_list]
    t_s = [_split(t) for t in inv]
    a_t = [_mm3(x, y) for x, y in zip(a_s, t_s)]
    yield
    out[:] = [t + _dot(ts[0], (eye - t - x).astype(BF16)) for t, ts, x in zip(inv, t_s, a_t)]


def _causal_conv(buf, x, conv_w):
    tt = x.shape[0]
    buf[HALO:HALO + tt, :] = x
    xx = buf[...]
    acc = conv_w[0:1, :] * xx
    for j in range(1, CONV_K):
        acc = conv_w[j:j + 1, :] * xx + pltpu.roll(acc, 1, 0)
    buf[0:HALO, :] = x[tt - HALO:tt, :]
    return acc[HALO:, :]


def _brows(b):
    return slice(b * CHUNK, (b + 1) * CHUNK)


def _hcols(h, base=0):
    return slice(base + h * HEAD_D, base + (h + 1) * HEAD_D)


def _inproj_kernel(x_ref, nw_ref, w_ref, o_ref, n_scr):
    @pl.when(pl.program_id(1) == 0)
    def _():
        n_scr[...] = _rms(x_ref[...], nw_ref[...]).astype(BF16)

    o_ref[...] = _dot(n_scr[...], w_ref[...], _NT)


def _inproj(x2d, norm_w, w_perm, layer, *, tm=1024, tn=1024):
    n_tok = x2d.shape[0]
    return pl.pallas_call(
        _inproj_kernel,
        out_shape=jax.ShapeDtypeStruct((n_tok, U_COLS), F32),
        grid=(n_tok // tm, U_COLS // tn),
        in_specs=[
            pl.BlockSpec((tm, D_MODEL), lambda i, j: (i, 0)),
            pl.BlockSpec((1, D_MODEL), lambda i, j: (0, 0)),
            pl.BlockSpec((None, tn, D_MODEL), lambda i, j: (layer, j, 0)),
        ],
        out_specs=pl.BlockSpec((tm, tn), lambda i, j: (i, j)),
        scratch_shapes=[pltpu.VMEM((tm, D_MODEL), BF16)],
        compiler_params=pltpu.CompilerParams(
            dimension_semantics=("parallel", "arbitrary"), vmem_limit_bytes=VMEM_LIMIT),
        name="inproj",
    )(x2d, norm_w.reshape(1, D_MODEL), w_perm)


def _gdn_steps(u_ref, cw_ref, prm_ref, o_ref, buf, s_ref):
    nb = u_ref.shape[0]
    cw = cw_ref[...]
    qkv = _silu(jnp.concatenate(
        [_causal_conv(buf.at[b], u_ref[b, :, GDN_C0:GDN_C0 + 3 * GROUP_W], cw) for b in range(nb)], axis=0))
    yield
    gates = u_ref[:, :, GATES_C0:GATES_C0 + LANES].reshape(nb * CHUNK, LANES)
    prm = prm_ref[...]
    g_all = -jnp.exp(prm[0:1, :]) * _softplus(gates + prm[1:2, :])
    beta_all = _sigmoid(gates)
    gc_all = _chunk_cumsum(g_all)
    norm_w = prm[2:3, :]
    gc_t = [gc_all[_brows(b), :].T for b in range(nb)]
    yield

    qn, kn = [], []
    for h in range(N_HEADS):
        q = qkv[:, _hcols(h)]
        k = qkv[:, _hcols(h, GROUP_W)]
        qn.append(q * (lax.rsqrt(jnp.sum(q * q, axis=-1, keepdims=True) + 1e-6) * (HEAD_D ** -0.5)))
        kn.append(k * lax.rsqrt(jnp.sum(k * k, axis=-1, keepdims=True) + 1e-6))
    yield

    ri = lax.broadcasted_iota(jnp.int32, (CHUNK, CHUNK), 0)
    ci = lax.broadcasted_iota(jnp.int32, (CHUNK, CHUNK), 1)
    causal = ri >= ci
    strict = ri > ci

    probs = [(b, h) for b in range(nb) for h in range(N_HEADS)]
    q = [qn[h][_brows(b), :] for b, h in probs]
    k = [kn[h][_brows(b), :] for b, h in probs]
    v = [qkv[_brows(b), _hcols(h, 2 * GROUP_W)] for b, h in probs]
    beta = [beta_all[_brows(b), N_HEADS + h:N_HEADS + h + 1] for b, h in probs]
    gcol = [gc_all[_brows(b), h:h + 1] for b, h in probs]
    grow = [gc_t[b][h:h + 1, :] for b, h in probs]
    decay = [jnp.where(causal, jnp.exp(jnp.where(causal, gc - gr, 0.0)), 0.0) for gc, gr in zip(gcol, grow)]
    yield
    kb = [x * y for x, y in zip(k, beta)]
    raw = [_mm(jnp.concatenate([x, y], axis=0), z, _NT) for x, y, z in zip(kb, q, k)]
    yield
    a_mat = [jnp.where(strict, r[:CHUNK] * d, 0.0) for r, d in zip(raw, decay)]
    attn = [r[CHUNK:] * d for r, d in zip(raw, decay)]
    t_mat = []
    yield from _inv_unit_lower_steps(a_mat, t_mat)
    eg = [jnp.exp(x) for x in gcol]
    sol = [_mm(t, jnp.concatenate([x * y, z * e], axis=1))
           for t, x, y, z, e in zip(t_mat, v, beta, kb, eg)]
    yield
    s_old = [s_ref[b, h] for b, h in probs]
    ws = [_mm(jnp.concatenate([x[:, HEAD_D:], y * e], axis=0), s)
          for x, y, e, s in zip(sol, q, eg, s_old)]
    yield
    v_new = [x[:, :HEAD_D] - y[:CHUNK] for x, y in zip(sol, ws)]
    o = [y[CHUNK:] + _mm(a, x) for y, a, x in zip(ws, attn, v_new)]
    yield
    glast = [x[CHUNK - 1:CHUNK, :] for x in gcol]
    s_new = [s * jnp.exp(gl) + _mm(x * jnp.exp(gl - gc), vn, _TN)
             for s, gl, x, gc, vn in zip(s_old, glast, k, gcol, v_new)]
    for (b, h), s in zip(probs, s_new):
        s_ref[b, h] = s
    yield
    for (b, h), x in zip(probs, o):
        z = u_ref[b, :, _hcols(h, GDN_C0 + 3 * GROUP_W)]
        o_ref[b, :, _hcols(h, OUT_GDN)] = (_rms(x, norm_w) * _silu(z)).astype(BF16)


def _mlstm_steps(u_ref, cw_ref, prm_ref, o_ref, buf, c_ref, n_ref, m_ref):
    nb = u_ref.shape[0]
    cw = cw_ref[...]
    qk = _silu(jnp.concatenate(
        [_causal_conv(buf.at[b], u_ref[b, :, MLSTM_C0:MLSTM_C0 + 2 * GROUP_W], cw) for b in range(nb)], axis=0))
    yield
    gates = u_ref[:, :, GATES_C0:GATES_C0 + LANES].reshape(nb * CHUNK, LANES)
    prm = prm_ref[...]
    pre = gates + prm[0:1, :]
    b_all = _chunk_cumsum(-_softplus(-pre))
    b_tr = [b_all[_brows(b), :].T for b in range(nb)]
    i_tr = [pre[_brows(b), :].T for b in range(nb)]
    yield

    ri = lax.broadcasted_iota(jnp.int32, (CHUNK, CHUNK), 0)
    ci = lax.broadcasted_iota(jnp.int32, (CHUNK, CHUNK), 1)
    causal = ri >= ci

    probs = [(b, h) for b in range(nb) for h in range(N_HEADS)]
    li = [2 * N_HEADS + h for _, h in probs]
    lf = [3 * N_HEADS + h for _, h in probs]
    q = [qk[_brows(b), _hcols(h)] for b, h in probs]
    k = [qk[_brows(b), _hcols(h, GROUP_W)] * (HEAD_D ** -0.5) for b, h in probs]
    v = [u_ref[b, :, _hcols(h, MLSTM_C0 + 2 * GROUP_W)] for b, h in probs]
    bcol = [b_all[_brows(b), l:l + 1] for (b, _), l in zip(probs, lf)]
    icol = [pre[_brows(b), l:l + 1] for (b, _), l in zip(probs, li)]
    brow = [b_tr[b][l:l + 1, :] for (b, _), l in zip(probs, lf)]
    irow = [i_tr[b][l:l + 1, :] for (b, _), l in zip(probs, li)]
    m_old = [m_ref[b, h][0:1, 0:1] for b, h in probs]
    c_old = [c_ref[b, h] for b, h in probs]
    n_old = [n_ref[b, h][0:1, :] for b, h in probs]
    yield

    d = [jnp.where(causal, bc - br + ir, -jnp.inf) for bc, br, ir in zip(bcol, brow, irow)]
    inter = [bc + m for bc, m in zip(bcol, m_old)]
    m_t = [jnp.maximum(x, jnp.max(y, axis=-1, keepdims=True)) for x, y in zip(inter, d)]
    yield
    qk_raw = [_mm(x, y, _NT) for x, y in zip(q, k)]
    qc = [_mm(x, c) for x, c in zip(q, c_old)]
    yield
    s = [r * jnp.exp(x - m) for r, x, m in zip(qk_raw, d, m_t)]
    a = [jnp.exp(x - m) for x, m in zip(inter, m_t)]
    yield
    sv = [_mm(x, y) for x, y in zip(s, v)]
    yield
    num = [x * y + z for x, y, z in zip(a, qc, sv)]
    den = [x * jnp.sum(y * n, axis=-1, keepdims=True) + jnp.sum(z, axis=-1, keepdims=True)
           for x, y, n, z in zip(a, q, n_old, s)]
    h_t = [x / jnp.maximum(jnp.abs(y), jnp.exp(-m)) for x, y, m in zip(num, den, m_t)]
    yield
    g = [x[CHUNK - 1:CHUNK, :] for x in bcol]
    w_log = [x - bc + ic for x, bc, ic in zip(g, bcol, icol)]
    m_new = [jnp.maximum(x + m, jnp.max(w, axis=0, keepdims=True)) for x, m, w in zip(g, m_old, w_log)]
    scale = [jnp.exp(x + m - mn) for x, m, mn in zip(g, m_old, m_new)]
    kw = [x * jnp.exp(w - mn) for x, w, mn in zip(k, w_log, m_new)]
    yield
    c_new = [sc * c + _mm(x, y, _TN) for sc, c, x, y in zip(scale, c_old, kw, v)]
    n_new = [sc * n + jnp.sum(x, axis=0, keepdims=True) for sc, n, x in zip(scale, n_old, kw)]
    for (b, h), c, n, m in zip(probs, c_new, n_new, m_new):
        c_ref[b, h] = c
        n_ref[b, h] = jnp.broadcast_to(n, (8, HEAD_D))
        m_ref[b, h] = jnp.broadcast_to(m, (8, HEAD_D))
    yield
    for (b, h), x in zip(probs, h_t):
        o_gate = u_ref[b, :, _hcols(h, MLSTM_C0 + 3 * GROUP_W)]
        o_ref[b, :, _hcols(h, OUT_MLSTM)] = _rms(_sigmoid(o_gate) * x, prm[1 + h:2 + h, :]).astype(BF16)


def _rglru_steps(u_ref, cw_ref, prm_ref, wa_ref, wx_ref, o_ref, buf, h_ref):
    nb = u_ref.shape[0]
    prm = prm_ref[...]
    cw = cw_ref[...]
    xb = jnp.concatenate(
        [_causal_conv(buf.at[b], u_ref[b, :, RGLRU_C0:RGLRU_C0 + GROUP_W], cw) for b in range(nb)], axis=0)
    xb = xb + prm[0:1, :]
    yield
    ra = []
    rx = []
    for n in range(N_HEADS):
        blk = xb[:, _hcols(n)].astype(BF16)
        ra.append(_dot(blk, wa_ref[n]))
        rx.append(_dot(blk, wx_ref[n]))
    yield
    r = _sigmoid(jnp.concatenate(ra, axis=1) + prm[1:2, :])
    i = _sigmoid(jnp.concatenate(rx, axis=1) + prm[2:3, :])
    log_a = -RGLRU_C * r * _softplus(-prm[3:4, :])
    a = jnp.exp(log_a)
    th = jnp.tanh(log_a)
    bb = jnp.sqrt(-2.0 * th / (1.0 - th)) * (i * xb)
    yield
    row = lax.broadcasted_iota(jnp.int32, a.shape, 0) % HALO
    d = 1
    while d < HALO:
        keep = row >= d
        bb = jnp.where(keep, a * pltpu.roll(bb, d, 0) + bb, bb)
        a = jnp.where(keep, a * pltpu.roll(a, d, 0), a)
        d *= 2
        yield
    carry = [h_ref[b][0:1, :] for b in range(nb)]
    groups = [[] for _ in range(nb)]
    for g in range(CHUNK // HALO):
        for b in range(nb):
            lo = b * CHUNK + g * HALO
            hg = bb[lo:lo + HALO, :] + a[lo:lo + HALO, :] * carry[b]
            groups[b].append(hg)
            carry[b] = hg[HALO - 1:HALO, :]
    yield
    for b in range(nb):
        h_ref[b] = jnp.broadcast_to(carry[b], (8, GROUP_W))
        gate = u_ref[b, :, RGLRU_C0 + GROUP_W:RGLRU_C0 + 2 * GROUP_W]
        gelu = 0.5 * gate * (1.0 + jnp.tanh(0.7978845608028654 * (gate + 0.044715 * gate * gate * gate)))
        o_ref[b, :, OUT_RGLRU:OUT_RGLRU + GROUP_W] = (jnp.concatenate(groups[b], axis=0) * gelu).astype(BF16)


def _pair_sum(x, lo_mask):
    s_lo = jnp.sum(jnp.where(lo_mask, x, 0.0), axis=-1, keepdims=True)
    s_hi = jnp.sum(jnp.where(lo_mask, 0.0, x), axis=-1, keepdims=True)
    return jnp.where(lo_mask, s_lo, s_hi)


def _stack_heads(x, lo_mask):
    return jnp.concatenate([jnp.where(lo_mask, x, 0.0), jnp.where(lo_mask, 0.0, x)], axis=0)


def _rwkv_steps(u_ref, mu_ref, prm_ref, wlo_ref, g2_ref, o_ref, buf, s_ref):
    nb = u_ref.shape[0]
    mu = mu_ref[...]
    xs = []
    for b in range(nb):
        x = u_ref[b, :, RWKV_C0:RWKV_C0 + RWKV_USED]
        buf[b, HALO:HALO + CHUNK, :] = x
        x_prev = buf[b, HALO - 1:HALO - 1 + CHUNK, :]
        buf[b, 0:HALO, :] = x[CHUNK - HALO:CHUNK, :]
        xs.append(x + (x_prev - x) * mu)
    xs = jnp.concatenate(xs, axis=0)
    yield
    prm = prm_ref[...]
    r_all = xs[:, 0:GROUP_W]
    k_all = xs[:, GROUP_W:2 * GROUP_W]
    v_all = xs[:, 2 * GROUP_W:3 * GROUP_W]
    lora = xs[:, 3 * GROUP_W:3 * GROUP_W + LANES]
    lane = lax.broadcasted_iota(jnp.int32, lora.shape, 1)
    lora = jnp.where(lane < RWKV_N, jnp.tanh(lora), lora)
    wa = _mm(lora, wlo_ref[...])
    g_all = _mm(_sigmoid(xs[:, 3 * GROUP_W + LANES:3 * GROUP_W + 2 * LANES]), g2_ref[...])
    yield
    w_raw = -_softplus(-(prm[0:1, :] + wa[:, :GROUP_W])) - 0.5
    lw_all = -jnp.exp(w_raw)
    a_all = _sigmoid(prm[1:2, :] + wa[:, GROUP_W:])
    kk_all = k_all * prm[2:3, :]
    k2_all = k_all * (1.0 + (a_all - 1.0) * prm[3:4, :])
    yield
    lc_all = _chunk_cumsum_mxu(lw_all)
    yield

    lo_mask = lax.broadcasted_iota(jnp.int32, (CHUNK, LANES), 1) < RWKV_N
    n2 = 2 * CHUNK
    ri = lax.broadcasted_iota(jnp.int32, (n2, n2), 0)
    ci = lax.broadcasted_iota(jnp.int32, (n2, n2), 1)
    same = (ri // CHUNK) == (ci // CHUNK)
    strict = same & ((ci % CHUNK) < (ri % CHUNK))
    incl = same & ((ci % CHUNK) <= (ri % CHUNK))

    probs = [(b, p) for b in range(nb) for p in range(N_PAIRS)]

    def sel(arr):
        return [arr[_brows(b), p * LANES:(p + 1) * LANES] for b, p in probs]

    r, v, k2, a_sig, lw, lc = sel(r_all), sel(v_all), sel(k2_all), sel(a_all), sel(lw_all), sel(lc_all)
    kk = [x * lax.rsqrt(_pair_sum(x * x, lo_mask) + 1e-6) for x in sel(kk_all)]
    yield
    w_incl = [jnp.exp(x) for x in lc]
    w_inv = [jnp.exp(-x) for x in lc]
    a_hat = [_stack_heads(-x * jnp.exp(c - w), lo_mask) for x, c, w in zip(kk, lc, lw)]
    b_hat = [_stack_heads(x * y * w, lo_mask) for x, y, w in zip(kk, a_sig, w_inv)]
    yield
    k_hat = [_stack_heads(x * w, lo_mask) for x, w in zip(k2, w_inv)]
    r_hat = [_stack_heads(x * w, lo_mask) for x, w in zip(r, w_incl)]
    v_st = [_stack_heads(x, lo_mask) for x in v]
    s_old = [s_ref[b, p] for b, p in probs]
    yield

    ar = [jnp.concatenate([x, y], axis=0).astype(BF16) for x, y in zip(a_hat, r_hat)]
    bk = [jnp.concatenate([x, y], axis=0).astype(BF16) for x, y in zip(b_hat, k_hat)]
    cross = [_dot(x, y, _NT) for x, y in zip(ar, bk)]
    yield
    a_ab = [jnp.where(strict, x[:n2, :n2], 0.0) for x in cross]
    a_ak = [jnp.where(strict, x[:n2, n2:], 0.0) for x in cross]
    m_r = [jnp.concatenate([jnp.where(incl, x[n2:, :n2], 0.0), jnp.where(incl, x[n2:, n2:], 0.0)], axis=1)
           for x in cross]
    yield
    ars = [_dot(x, s.astype(BF16), _NT) for x, s in zip(ar, s_old)]
    akv = [_mm(y, z) for y, z in zip(a_ak, v_st)]
    t_mat = []
    yield from _inv_unit_lower_steps([-x for x in a_ab], t_mat)
    pv = [_mm(t, x[:n2] + y) for t, x, y in zip(t_mat, ars, akv)]
    yield
    pvv = [jnp.concatenate([x, y], axis=0).astype(BF16) for x, y in zip(pv, v_st)]
    y_st = [x[n2:] + _dot(m.astype(BF16), z) for x, m, z in zip(ars, m_r, pvv)]
    yield
    s_new = [(s + _dot(x, y, _TN)) * w[CHUNK - 1:CHUNK, :] for s, x, y, w in zip(s_old, pvv, bk, w_incl)]
    for (b, p), s in zip(probs, s_new):
        s_ref[b, p] = s
    yield
    for (b, p), ys, rr, kk2, vv in zip(probs, y_st, r, k2, v):
        cols = slice(p * LANES, (p + 1) * LANES)
        y = ys[:CHUNK, :] + ys[CHUNK:, :]
        yc = y - _pair_sum(y, lo_mask) * (1.0 / RWKV_N)
        var = _pair_sum(yc * yc, lo_mask) * (1.0 / RWKV_N)
        y = yc * lax.rsqrt(var + RWKV_LN_EPS) * prm[5:6, cols] + prm[6:7, cols]
        y = y + _pair_sum(rr * kk2 * prm[4:5, cols], lo_mask) * vv
        o_ref[b, :, OUT_RWKV + p * LANES:OUT_RWKV + (p + 1) * LANES] = (y * g_all[_brows(b), cols]).astype(BF16)


def _mixers_kernel(u_ref, gdn_cw, gdn_prm, ml_cw, ml_prm, rg_cw, rg_prm, rg_wa, rg_wx, rw_mu, rw_prm, rw_wlo, rw_g2,
                   o_ref, gdn_buf, gdn_s, ml_buf, ml_c, ml_n, ml_m, rg_buf, rg_h, rw_buf, rw_s):
    @pl.when(pl.program_id(0) == 0)
    def _():
        for buf in (gdn_buf, ml_buf, rg_buf, rw_buf):
            buf[:, 0:HALO, :] = jnp.zeros((buf.shape[0], HALO, buf.shape[2]), F32)
        for state in (gdn_s, ml_c, ml_n, ml_m, rg_h, rw_s):
            state[...] = jnp.zeros_like(state)

    stages = [
        _rwkv_steps(u_ref, rw_mu, rw_prm, rw_wlo, rw_g2, o_ref, rw_buf, rw_s),
        _gdn_steps(u_ref, gdn_cw, gdn_prm, o_ref, gdn_buf, gdn_s),
        _mlstm_steps(u_ref, ml_cw, ml_prm, o_ref, ml_buf, ml_c, ml_n, ml_m),
        _rglru_steps(u_ref, rg_cw, rg_prm, rg_wa, rg_wx, o_ref, rg_buf, rg_h),
    ]
    for _ in itertools.zip_longest(*stages):
        pass


def _mixers(u, gdn_cw, gdn_prm, ml_cw, ml_prm, rg_cw, rg_prm, rg_wa, rg_wx, rw_mu, rw_prm, rw_wlo, rw_g2):
    bsz, seq, _ = u.shape

    def whole(a):
        return pl.BlockSpec(a.shape, lambda t, _n=a.ndim: (0,) * _n)

    params = (gdn_cw, gdn_prm, ml_cw, ml_prm, rg_cw, rg_prm, rg_wa, rg_wx, rw_mu, rw_prm, rw_wlo, rw_g2)
    return pl.pallas_call(
        _mixers_kernel,
        out_shape=jax.ShapeDtypeStruct((bsz, seq, D_MODEL), BF16),
        grid=(seq // CHUNK,),
        in_specs=[pl.BlockSpec((bsz, CHUNK, U_COLS), lambda t: (0, t, 0))] + [whole(a) for a in params],
        out_specs=pl.BlockSpec((bsz, CHUNK, D_MODEL), lambda t: (0, t, 0)),
        scratch_shapes=[
            pltpu.VMEM((bsz, HALO + CHUNK, 3 * GROUP_W), F32),
            pltpu.VMEM((bsz, N_HEADS, HEAD_D, HEAD_D), F32),
            pltpu.VMEM((bsz, HALO + CHUNK, 2 * GROUP_W), F32),
            pltpu.VMEM((bsz, N_HEADS, HEAD_D, HEAD_D), F32),
            pltpu.VMEM((bsz, N_HEADS, 8, HEAD_D), F32),
            pltpu.VMEM((bsz, N_HEADS, 8, HEAD_D), F32),
            pltpu.VMEM((bsz, HALO + CHUNK, GROUP_W), F32),
            pltpu.VMEM((bsz, 8, GROUP_W), F32),
            pltpu.VMEM((bsz, HALO + CHUNK, RWKV_USED), F32),
            pltpu.VMEM((bsz, N_PAIRS, LANES, LANES), F32),
        ],
        compiler_params=pltpu.CompilerParams(
            dimension_semantics=("arbitrary",), vmem_limit_bytes=VMEM_LIMIT),
        name="mixers",
    )(u, *params)


def _outproj_kernel(x_ref, y_ref, w_ref, o_ref):
    o_ref[...] = x_ref[...] + _dot(y_ref[...], w_ref[...])


def _outproj(x2d, mix, w_out, layer, *, tm=512):
    n_tok = x2d.shape[0]
    return pl.pallas_call(
        _outproj_kernel,
        out_shape=jax.ShapeDtypeStruct((n_tok, D_MODEL), F32),
        grid=(n_tok // tm,),
        in_specs=[pl.BlockSpec((tm, D_MODEL), lambda i: (i, 0)),
                  pl.BlockSpec((tm, D_MODEL), lambda i: (i, 0)),
                  pl.BlockSpec((None, D_MODEL, D_MODEL), lambda i: (layer, 0, 0))],
        out_specs=pl.BlockSpec((tm, D_MODEL), lambda i: (i, 0)),
        compiler_params=pltpu.CompilerParams(
            dimension_semantics=("parallel",), vmem_limit_bytes=VMEM_LIMIT),
        name="outproj",
    )(x2d, mix, w_out)


def _ffn_kernel(x_ref, nw_ref, wg_ref, wu_ref, wd_ref, fw_ref, o_ref, n_scr, *, n_f, final):
    f = pl.program_id(1)

    @pl.when(f == 0)
    def _():
        x = x_ref[...]
        n_scr[...] = _rms(x, nw_ref[...]).astype(BF16)
        o_ref[...] = x

    n = n_scr[...]
    gate = jnp.dot(n, wg_ref[...], preferred_element_type=F32)
    up = jnp.dot(n, wu_ref[...], preferred_element_type=F32)
    hid = (_silu(gate) * up).astype(BF16)
    o_ref[...] += jnp.dot(hid, wd_ref[...], preferred_element_type=F32)

    if final:
        @pl.when(f == n_f - 1)
        def _():
            o_ref[...] = _rms(o_ref[...], fw_ref[...])


def _ffn(x2d, norm_w, w_gate, w_up, w_down, final_w, layer, *, final, tm=1024, tf=512):
    n_tok = x2d.shape[0]
    n_f = FFN_HIDDEN // tf
    return pl.pallas_call(
        functools.partial(_ffn_kernel, n_f=n_f, final=final),
        out_shape=jax.ShapeDtypeStruct((n_tok, D_MODEL), F32),
        grid=(n_tok // tm, n_f),
        in_specs=[
            pl.BlockSpec((tm, D_MODEL), lambda i, f: (i, 0)),
            pl.BlockSpec((1, D_MODEL), lambda i, f: (0, 0)),
            pl.BlockSpec((None, D_MODEL, tf), lambda i, f: (layer, 0, f)),
            pl.BlockSpec((None, D_MODEL, tf), lambda i, f: (layer, 0, f)),
            pl.BlockSpec((None, tf, D_MODEL), lambda i, f: (layer, f, 0)),
            pl.BlockSpec((1, D_MODEL), lambda i, f: (0, 0)),
        ],
        out_specs=pl.BlockSpec((tm, D_MODEL), lambda i, f: (i, 0)),
        scratch_shapes=[pltpu.VMEM((tm, D_MODEL), BF16)],
        compiler_params=pltpu.CompilerParams(
            dimension_semantics=("parallel", "arbitrary"), vmem_limit_bytes=VMEM_LIMIT),
        name="ffn_final" if final else "ffn",
    )(x2d, norm_w.reshape(1, D_MODEL), w_gate, w_up, w_down, final_w.reshape(1, D_MODEL))


def _pad_lanes(v, offset, width):
    v = v.astype(F32)
    return jnp.concatenate([jnp.zeros((offset,), F32), v, jnp.zeros((width - offset - v.shape[0],), F32)])


def _rows(rows, width):
    rows = [r.astype(F32).reshape(width) for r in rows]
    return jnp.stack(rows + [jnp.zeros((width,), F32)] * (8 - len(rows)))


def _perm_w_in(w):
    wt = jnp.swapaxes(w, 1, 2).astype(BF16)
    pad = jnp.zeros((w.shape[0], RGLRU_C0 - GATES_C0 - 16, w.shape[1]), BF16)
    return jnp.concatenate([wt[:, 0:2048], wt[:, 2056:4104], wt[:, 5136:6928], wt[:, 2048:2056], wt[:, 4104:4112],
                            pad, wt[:, 4112:5136]], axis=1)


def kernel(x, attn_norm, w_in, gdn_conv, gdn_a_log, gdn_dt_bias, gdn_norm, mlstm_conv, mlstm_b_i, mlstm_b_f, mlstm_norm, rglru_conv, rglru_conv_b, rglru_w_a, rglru_b_a, rglru_w_x, rglru_b_x, rglru_lambda, rwkv_mu, rwkv_w0, rwkv_w2, rwkv_a0, rwkv_a2, rwkv_g2, rwkv_k_k, rwkv_k_a, rwkv_r_k, rwkv_ln_w, rwkv_ln_b, w_out, ffn_norm, ffn_w_gate, ffn_w_up, ffn_w_down, final_norm):
    bsz, seq, d = x.shape
    depth = w_in.shape[0]
    x2d = x.reshape(bsz * seq, d)
    w_perm = _perm_w_in(w_in)
    w_out_bf, w_gate_bf, w_up_bf, w_down_bf = [w.astype(BF16) for w in (w_out, ffn_w_gate, ffn_w_up, ffn_w_down)]
    for l in range(depth):
        u = _inproj(x2d, attn_norm[l], w_perm, l).reshape(bsz, seq, U_COLS)

        gdn_prm = _rows([_pad_lanes(gdn_a_log[l], 0, LANES), _pad_lanes(gdn_dt_bias[l], 0, LANES),
                         gdn_norm[l]], LANES)
        ml_bias = _pad_lanes(mlstm_b_i[l], 2 * N_HEADS, LANES) + _pad_lanes(mlstm_b_f[l], 3 * N_HEADS, LANES)
        ml_prm = _rows([ml_bias] + [mlstm_norm[l, h] for h in range(N_HEADS)], LANES)
        rg_prm = _rows([rglru_conv_b[l], rglru_b_a[l], rglru_b_x[l], rglru_lambda[l]], GROUP_W)
        rw_prm = _rows([rwkv_w0[l], rwkv_a0[l], rwkv_k_k[l], rwkv_k_a[l], rwkv_r_k[l],
                        rwkv_ln_w[l], rwkv_ln_b[l]], GROUP_W)
        zeros_lora = jnp.zeros((RWKV_N, GROUP_W), F32)
        w_lora = jnp.concatenate([
            jnp.concatenate([rwkv_w2[l], zeros_lora], axis=1),
            jnp.concatenate([zeros_lora, rwkv_a2[l]], axis=1)], axis=0).astype(BF16)
        mix = _mixers(u, gdn_conv[l], gdn_prm, mlstm_conv[l], ml_prm, rglru_conv[l], rg_prm,
                      rglru_w_a[l].astype(BF16), rglru_w_x[l].astype(BF16),
                      rwkv_mu[l].reshape(1, RWKV_USED), rw_prm, w_lora, rwkv_g2[l].astype(BF16))

        x2d = _outproj(x2d, mix.reshape(bsz * seq, d), w_out_bf, l)
        x2d = _ffn(x2d, ffn_norm[l], w_gate_bf, w_up_bf, w_down_bf, final_norm, l, final=(l == depth - 1))
    return x2d.reshape(bsz, seq, d)
```

```python
import functools
import itertools

import jax
import jax.numpy as jnp
from jax import lax
from jax.experimental import pallas as pl
from jax.experimental.pallas import tpu as pltpu

F32 = jnp.float32
BF16 = jnp.bfloat16

D_MODEL = 2048
GROUP_W = 512
HEAD_D = 128
N_HEADS = 4
RWKV_N = 64
N_PAIRS = GROUP_W // (2 * RWKV_N)
CONV_K = 4
CHUNK = 64
FFN_HIDDEN = 5632
NORM_EPS = 1e-6
RWKV_LN_EPS = 64e-5
RGLRU_C = 8.0
LANES = 128
HALO = 8
INV_LEVELS = 5

U_COLS = 7168
GDN_C0 = 0
MLSTM_C0 = 2048
RWKV_C0 = 4096
RWKV_USED = 1792
GATES_C0 = 5888
RGLRU_C0 = 6144
OUT_GDN, OUT_MLSTM, OUT_RGLRU, OUT_RWKV = 0, GROUP_W, 2 * GROUP_W, 3 * GROUP_W

VMEM_LIMIT = 60 * 1024 * 1024

_NN = (((1,), (0,)), ((), ()))
_NT = (((1,), (1,)), ((), ()))
_TN = (((0,), (0,)), ((), ()))


def _dot(a, b, dims=_NN):
    return lax.dot_general(a, b, dims, preferred_element_type=F32)


def _mm(a, b, dims=_NN):
    return _dot(a.astype(BF16), b.astype(BF16), dims)


def _sigmoid(x):
    return 1.0 / (1.0 + jnp.exp(-x))


def _softplus(x):
    return jnp.maximum(x, 0.0) + jnp.log(1.0 + jnp.exp(-jnp.abs(x)))


def _silu(x):
    return x * _sigmoid(x)


def _rms(x, w, eps=NORM_EPS):
    return x * lax.rsqrt(jnp.mean(x * x, axis=-1, keepdims=True) + eps) * w


def _chunk_cumsum(x):
    row = lax.broadcasted_iota(jnp.int32, x.shape, 0) % CHUNK
    d = 1
    while d < CHUNK:
        x = x + jnp.where(row >= d, pltpu.roll(x, d, 0), 0.0)
        d *= 2
    return x


def _chunk_cumsum_mxu(x):
    ri = lax.broadcasted_iota(jnp.int32, (CHUNK, CHUNK), 0)
    ci = lax.broadcasted_iota(jnp.int32, (CHUNK, CHUNK), 1)
    tri = (ri >= ci).astype(BF16)
    hi = x.astype(BF16)
    r1 = x - hi.astype(F32)
    mid = r1.astype(BF16)
    lo = (r1 - mid.astype(F32)).astype(BF16)
    outs = []
    for c in range(x.shape[0] // CHUNK):
        rows = _brows(c)
        outs.append(_dot(tri, hi[rows, :]) + (_dot(tri, mid[rows, :]) + _dot(tri, lo[rows, :])))
    return jnp.concatenate(outs, axis=0)


def _split(a):
    hi = a.astype(BF16)
    return hi, (a - hi.astype(F32)).astype(BF16)


def _mm3(a, b):
    return _dot(a[0], b[0]) + (_dot(a[0], b[1]) + _dot(a[1], b[0]))


def _inv_unit_lower_steps(a_list, out):
    n = a_list[0].shape[0]
    eye = (lax.broadcasted_iota(jnp.int32, (n, n), 0) == lax.broadcasted_iota(jnp.int32, (n, n), 1)).astype(F32)
    inv = [eye - a for a in a_list]
    p = [(-a).astype(BF16) for a in a_list]
    for _ in range(INV_LEVELS):
        p = [_dot(x, x).astype(BF16) for x in p]
        yield
        inv = [x + _dot(x.astype(BF16), y) for x, y in zip(inv, p)]
        yield
    a_s = [_split(a) for a in a_list]
    t_s = [_split(t) for t in inv]
    a_t = [_mm3(x, y) for x, y in zip(a_s, t_s)]
    yield
    out[:] = [t + _dot(ts[0], (eye - t - x).astype(BF16)) for t, ts, x in zip(inv, t_s, a_t)]


def _causal_conv(buf, x, conv_w):
    tt = x.shape[0]
    buf[HALO:HALO + tt, :] = x
    xx = buf[...]
    acc = conv_w[0:1, :] * xx
    for j in range(1, CONV_K):
        acc = conv_w[j:j + 1, :] * xx + pltpu.roll(acc, 1, 0)
    buf[0:HALO, :] = x[tt - HALO:tt, :]
    return acc[HALO:, :]


def _brows(b):
    return slice(b * CHUNK, (b + 1) * CHUNK)


def _hcols(h, base=0):
    return slice(base + h * HEAD_D, base + (h + 1) * HEAD_D)


def _inproj_kernel(x_ref, nw_ref, w_ref, o_ref, n_scr):
    @pl.when(pl.program_id(1) == 0)
    def _():
        n_scr[...] = _rms(x_ref[...], nw_ref[...]).astype(BF16)

    o_ref[...] = _dot(n_scr[...], w_ref[...], _NT)


def _inproj(x2d, norm_w, w_perm, layer, *, tm=1024, tn=1792):
    n_tok = x2d.shape[0]
    return pl.pallas_call(
        _inproj_kernel,
        out_shape=jax.ShapeDtypeStruct((n_tok, U_COLS), F32),
        grid=(n_tok // tm, U_COLS // tn),
        in_specs=[
            pl.BlockSpec((tm, D_MODEL), lambda i, j: (i, 0)),
            pl.BlockSpec((1, D_MODEL), lambda i, j: (0, 0)),
            pl.BlockSpec((None, tn, D_MODEL), lambda i, j: (layer, j, 0)),
        ],
        out_specs=pl.BlockSpec((tm, tn), lambda i, j: (i, j)),
        scratch_shapes=[pltpu.VMEM((tm, D_MODEL), BF16)],
        compiler_params=pltpu.CompilerParams(
            dimension_semantics=("parallel", "arbitrary"), vmem_limit_bytes=VMEM_LIMIT),
        name="inproj",
    )(x2d, norm_w.reshape(1, D_MODEL), w_perm)


def _gdn_steps(u_ref, cw_ref, prm_ref, o_ref, buf, s_ref):
    nb = u_ref.shape[0]
    cw = cw_ref[...]
    qkv = _silu(jnp.concatenate(
        [_causal_conv(buf.at[b], u_ref[b, :, GDN_C0:GDN_C0 + 3 * GROUP_W], cw) for b in range(nb)], axis=0))
    yield
    gates = u_ref[:, :, GATES_C0:GATES_C0 + LANES].reshape(nb * CHUNK, LANES)
    prm = prm_ref[...]
    g_all = -jnp.exp(prm[0:1, :]) * _softplus(gates + prm[1:2, :])
    beta_all = _sigmoid(gates)
    gc_all = _chunk_cumsum(g_all)
    norm_w = prm[2:3, :]
    gc_t = [gc_all[_brows(b), :].T for b in range(nb)]
    yield

    qn, kn = [], []
    for h in range(N_HEADS):
        q = qkv[:, _hcols(h)]
        k = qkv[:, _hcols(h, GROUP_W)]
        qn.append(q * (lax.rsqrt(jnp.sum(q * q, axis=-1, keepdims=True) + 1e-6) * (HEAD_D ** -0.5)))
        kn.append(k * lax.rsqrt(jnp.sum(k * k, axis=-1, keepdims=True) + 1e-6))
    yield

    ri = lax.broadcasted_iota(jnp.int32, (CHUNK, CHUNK), 0)
    ci = lax.broadcasted_iota(jnp.int32, (CHUNK, CHUNK), 1)
    causal = ri >= ci
    strict = ri > ci

    probs = [(b, h) for b in range(nb) for h in range(N_HEADS)]
    q = [qn[h][_brows(b), :] for b, h in probs]
    k = [kn[h][_brows(b), :] for b, h in probs]
    v = [qkv[_brows(b), _hcols(h, 2 * GROUP_W)] for b, h in probs]
    beta = [beta_all[_brows(b), N_HEADS + h:N_HEADS + h + 1] for b, h in probs]
    gcol = [gc_all[_brows(b), h:h + 1] for b, h in probs]
    grow = [gc_t[b][h:h + 1, :] for b, h in probs]
    decay = [jnp.where(causal, jnp.exp(jnp.where(causal, gc - gr, 0.0)), 0.0) for gc, gr in zip(gcol, grow)]
    yield
    kb = [x * y for x, y in zip(k, beta)]
    raw = [_mm(jnp.concatenate([x, y], axis=0), z, _NT) for x, y, z in zip(kb, q, k)]
    yield
    a_mat = [jnp.where(strict, r[:CHUNK] * d, 0.0) for r, d in zip(raw, decay)]
    attn = [r[CHUNK:] * d for r, d in zip(raw, decay)]
    t_mat = []
    yield from _inv_unit_lower_steps(a_mat, t_mat)
    eg = [jnp.exp(x) for x in gcol]
    sol = [_mm(t, jnp.concatenate([x * y, z * e], axis=1))
           for t, x, y, z, e in zip(t_mat, v, beta, kb, eg)]
    yield
    s_old = [s_ref[b, h] for b, h in probs]
    ws = [_mm(jnp.concatenate([x[:, HEAD_D:], y * e], axis=0), s)
          for x, y, e, s in zip(sol, q, eg, s_old)]
    yield
    v_new = [x[:, :HEAD_D] - y[:CHUNK] for x, y in zip(sol, ws)]
    o = [y[CHUNK:] + _mm(a, x) for y, a, x in zip(ws, attn, v_new)]
    yield
    glast = [x[CHUNK - 1:CHUNK, :] for x in gcol]
    s_new = [s * jnp.exp(gl) + _mm(x * jnp.exp(gl - gc), vn, _TN)
             for s, gl, x, gc, vn in zip(s_old, glast, k, gcol, v_new)]
    for (b, h), s in zip(probs, s_new):
        s_ref[b, h] = s
    yield
    for (b, h), x in zip(probs, o):
        z = u_ref[b, :, _hcols(h, GDN_C0 + 3 * GROUP_W)]
        o_ref[b, :, _hcols(h, OUT_GDN)] = (_rms(x, norm_w) * _silu(z)).astype(BF16)


def _mlstm_steps(u_ref, cw_ref, prm_ref, o_ref, buf, c_ref, n_ref, m_ref):
    nb = u_ref.shape[0]
    cw = cw_ref[...]
    qk = _silu(jnp.concatenate(
        [_causal_conv(buf.at[b], u_ref[b, :, MLSTM_C0:MLSTM_C0 + 2 * GROUP_W], cw) for b in range(nb)], axis=0))
    yield
    gates = u_ref[:, :, GATES_C0:GATES_C0 + LANES].reshape(nb * CHUNK, LANES)
    prm = prm_ref[...]
    pre = gates + prm[0:1, :]
    b_all = _chunk_cumsum(-_softplus(-pre))
    b_tr = [b_all[_brows(b), :].T for b in range(nb)]
    i_tr = [pre[_brows(b), :].T for b in range(nb)]
    yield

    ri = lax.broadcasted_iota(jnp.int32, (CHUNK, CHUNK), 0)
    ci = lax.broadcasted_iota(jnp.int32, (CHUNK, CHUNK), 1)
    causal = ri >= ci

    probs = [(b, h) for b in range(nb) for h in range(N_HEADS)]
    li = [2 * N_HEADS + h for _, h in probs]
    lf = [3 * N_HEADS + h for _, h in probs]
    q = [qk[_brows(b), _hcols(h)] for b, h in probs]
    k = [qk[_brows(b), _hcols(h, GROUP_W)] * (HEAD_D ** -0.5) for b, h in probs]
    v = [u_ref[b, :, _hcols(h, MLSTM_C0 + 2 * GROUP_W)] for b, h in probs]
    bcol = [b_all[_brows(b), l:l + 1] for (b, _), l in zip(probs, lf)]
    icol = [pre[_brows(b), l:l + 1] for (b, _), l in zip(probs, li)]
    brow = [b_tr[b][l:l + 1, :] for (b, _), l in zip(probs, lf)]
    irow = [i_tr[b][l:l + 1, :] for (b, _), l in zip(probs, li)]
    m_old = [m_ref[b, h][0:1, 0:1] for b, h in probs]
    c_old = [c_ref[b, h] for b, h in probs]
    n_old = [n_ref[b, h][0:1, :] for b, h in probs]
    yield

    d = [jnp.where(causal, bc - br + ir, -jnp.inf) for bc, br, ir in zip(bcol, brow, irow)]
    inter = [bc + m for bc, m in zip(bcol, m_old)]
    m_t = [jnp.maximum(x, jnp.max(y, axis=-1, keepdims=True)) for x, y in zip(inter, d)]
    yield
    qk_raw = [_mm(x, y, _NT) for x, y in zip(q, k)]
    qc = [_mm(x, c) for x, c in zip(q, c_old)]
    yield
    s = [r * jnp.exp(x - m) for r, x, m in zip(qk_raw, d, m_t)]
    a = [jnp.exp(x - m) for x, m in zip(inter, m_t)]
    yield
    sv = [_mm(x, y) for x, y in zip(s, v)]
    yield
    num = [x * y + z for x, y, z in zip(a, qc, sv)]
    den = [x * jnp.sum(y * n, axis=-1, keepdims=True) + jnp.sum(z, axis=-1, keepdims=True)
           for x, y, n, z in zip(a, q, n_old, s)]
    h_t = [x / jnp.maximum(jnp.abs(y), jnp.exp(-m)) for x, y, m in zip(num, den, m_t)]
    yield
    g = [x[CHUNK - 1:CHUNK, :] for x in bcol]
    w_log = [x - bc + ic for x, bc, ic in zip(g, bcol, icol)]
    m_new = [jnp.maximum(x + m, jnp.max(w, axis=0, keepdims=True)) for x, m, w in zip(g, m_old, w_log)]
    scale = [jnp.exp(x + m - mn) for x, m, mn in zip(g, m_old, m_new)]
    kw = [x * jnp.exp(w - mn) for x, w, mn in zip(k, w_log, m_new)]
    yield
    c_new = [sc * c + _mm(x, y, _TN) for sc, c, x, y in zip(scale, c_old, kw, v)]
    n_new = [sc * n + jnp.sum(x, axis=0, keepdims=True) for sc, n, x in zip(scale, n_old, kw)]
    for (b, h), c, n, m in zip(probs, c_new, n_new, m_new):
        c_ref[b, h] = c
        n_ref[b, h] = jnp.broadcast_to(n, (8, HEAD_D))
        m_ref[b, h] = jnp.broadcast_to(m, (8, HEAD_D))
    yield
    for (b, h), x in zip(probs, h_t):
        o_gate = u_ref[b, :, _hcols(h, MLSTM_C0 + 3 * GROUP_W)]
        o_ref[b, :, _hcols(h, OUT_MLSTM)] = _rms(_sigmoid(o_gate) * x, prm[1 + h:2 + h, :]).astype(BF16)


def _rglru_steps(u_ref, cw_ref, prm_ref, wa_ref, wx_ref, o_ref, buf, h_ref):
    nb = u_ref.shape[0]
    prm = prm_ref[...]
    cw = cw_ref[...]
    xb = jnp.concatenate(
        [_causal_conv(buf.at[b], u_ref[b, :, RGLRU_C0:RGLRU_C0 + GROUP_W], cw) for b in range(nb)], axis=0)
    xb = xb + prm[0:1, :]
    yield
    ra = []
    rx = []
    for n in range(N_HEADS):
        blk = xb[:, _hcols(n)].astype(BF16)
        ra.append(_dot(blk, wa_ref[n]))
        rx.append(_dot(blk, wx_ref[n]))
    yield
    r = _sigmoid(jnp.concatenate(ra, axis=1) + prm[1:2, :])
    i = _sigmoid(jnp.concatenate(rx, axis=1) + prm[2:3, :])
    log_a = -RGLRU_C * r * _softplus(-prm[3:4, :])
    a = jnp.exp(log_a)
    th = jnp.tanh(log_a)
    bb = jnp.sqrt(-2.0 * th / (1.0 - th)) * (i * xb)
    yield
    row = lax.broadcasted_iota(jnp.int32, a.shape, 0) % HALO
    d = 1
    while d < HALO:
        keep = row >= d
        bb = jnp.where(keep, a * pltpu.roll(bb, d, 0) + bb, bb)
        a = jnp.where(keep, a * pltpu.roll(a, d, 0), a)
        d *= 2
        yield
    carry = [h_ref[b][0:1, :] for b in range(nb)]
    groups = [[] for _ in range(nb)]
    for g in range(CHUNK // HALO):
        for b in range(nb):
            lo = b * CHUNK + g * HALO
            hg = bb[lo:lo + HALO, :] + a[lo:lo + HALO, :] * carry[b]
            groups[b].append(hg)
            carry[b] = hg[HALO - 1:HALO, :]
    yield
    for b in range(nb):
        h_ref[b] = jnp.broadcast_to(carry[b], (8, GROUP_W))
        gate = u_ref[b, :, RGLRU_C0 + GROUP_W:RGLRU_C0 + 2 * GROUP_W]
        gelu = 0.5 * gate * (1.0 + jnp.tanh(0.7978845608028654 * (gate + 0.044715 * gate * gate * gate)))
        o_ref[b, :, OUT_RGLRU:OUT_RGLRU + GROUP_W] = (jnp.concatenate(groups[b], axis=0) * gelu).astype(BF16)


def _pair_sum(x, lo_mask):
    s_lo = jnp.sum(jnp.where(lo_mask, x, 0.0), axis=-1, keepdims=True)
    s_hi = jnp.sum(jnp.where(lo_mask, 0.0, x), axis=-1, keepdims=True)
    return jnp.where(lo_mask, s_lo, s_hi)


def _stack_heads(x, lo_mask):
    return jnp.concatenate([jnp.where(lo_mask, x, 0.0), jnp.where(lo_mask, 0.0, x)], axis=0)


def _rwkv_steps(u_ref, mu_ref, prm_ref, wlo_ref, g2_ref, o_ref, buf, s_ref):
    nb = u_ref.shape[0]
    mu = mu_ref[...]
    xs = []
    for b in range(nb):
        x = u_ref[b, :, RWKV_C0:RWKV_C0 + RWKV_USED]
        buf[b, HALO:HALO + CHUNK, :] = x
        x_prev = buf[b, HALO - 1:HALO - 1 + CHUNK, :]
        buf[b, 0:HALO, :] = x[CHUNK - HALO:CHUNK, :]
        xs.append(x + (x_prev - x) * mu)
    xs = jnp.concatenate(xs, axis=0)
    yield
    prm = prm_ref[...]
    r_all = xs[:, 0:GROUP_W]
    k_all = xs[:, GROUP_W:2 * GROUP_W]
    v_all = xs[:, 2 * GROUP_W:3 * GROUP_W]
    lora = xs[:, 3 * GROUP_W:3 * GROUP_W + LANES]
    lane = lax.broadcasted_iota(jnp.int32, lora.shape, 1)
    lora = jnp.where(lane < RWKV_N, jnp.tanh(lora), lora)
    wa = _mm(lora, wlo_ref[...])
    g_all = _mm(_sigmoid(xs[:, 3 * GROUP_W + LANES:3 * GROUP_W + 2 * LANES]), g2_ref[...])
    yield
    w_raw = -_softplus(-(prm[0:1, :] + wa[:, :GROUP_W])) - 0.5
    lw_all = -jnp.exp(w_raw)
    a_all = _sigmoid(prm[1:2, :] + wa[:, GROUP_W:])
    kk_all = k_all * prm[2:3, :]
    k2_all = k_all * (1.0 + (a_all - 1.0) * prm[3:4, :])
    yield
    lc_all = _chunk_cumsum_mxu(lw_all)
    yield

    lo_mask = lax.broadcasted_iota(jnp.int32, (CHUNK, LANES), 1) < RWKV_N
    n2 = 2 * CHUNK
    ri = lax.broadcasted_iota(jnp.int32, (n2, n2), 0)
    ci = lax.broadcasted_iota(jnp.int32, (n2, n2), 1)
    same = (ri // CHUNK) == (ci // CHUNK)
    strict = same & ((ci % CHUNK) < (ri % CHUNK))
    incl = same & ((ci % CHUNK) <= (ri % CHUNK))

    probs = [(b, p) for b in range(nb) for p in range(N_PAIRS)]

    def sel(arr):
        return [arr[_brows(b), p * LANES:(p + 1) * LANES] for b, p in probs]

    r, v, k2, a_sig, lw, lc = sel(r_all), sel(v_all), sel(k2_all), sel(a_all), sel(lw_all), sel(lc_all)
    kk = [x * lax.rsqrt(_pair_sum(x * x, lo_mask) + 1e-6) for x in sel(kk_all)]
    yield
    w_incl = [jnp.exp(x) for x in lc]
    w_inv = [jnp.exp(-x) for x in lc]
    a_hat = [_stack_heads(-x * jnp.exp(c - w), lo_mask) for x, c, w in zip(kk, lc, lw)]
    b_hat = [_stack_heads(x * y * w, lo_mask) for x, y, w in zip(kk, a_sig, w_inv)]
    yield
    k_hat = [_stack_heads(x * w, lo_mask) for x, w in zip(k2, w_inv)]
    r_hat = [_stack_heads(x * w, lo_mask) for x, w in zip(r, w_incl)]
    v_st = [_stack_heads(x, lo_mask) for x in v]
    s_old = [s_ref[b, p] for b, p in probs]
    yield

    ar = [jnp.concatenate([x, y], axis=0).astype(BF16) for x, y in zip(a_hat, r_hat)]
    bk = [jnp.concatenate([x, y], axis=0).astype(BF16) for x, y in zip(b_hat, k_hat)]
    cross = [_dot(x, y, _NT) for x, y in zip(ar, bk)]
    yield
    a_ab = [jnp.where(strict, x[:n2, :n2], 0.0) for x in cross]
    a_ak = [jnp.where(strict, x[:n2, n2:], 0.0) for x in cross]
    m_r = [jnp.concatenate([jnp.where(incl, x[n2:, :n2], 0.0), jnp.where(incl, x[n2:, n2:], 0.0)], axis=1)
           for x in cross]
    yield
    ars = [_dot(x, s.astype(BF16), _NT) for x, s in zip(ar, s_old)]
    akv = [_mm(y, z) for y, z in zip(a_ak, v_st)]
    t_mat = []
    yield from _inv_unit_lower_steps([-x for x in a_ab], t_mat)
    pv = [_mm(t, x[:n2] + y) for t, x, y in zip(t_mat, ars, akv)]
    yield
    pvv = [jnp.concatenate([x, y], axis=0).astype(BF16) for x, y in zip(pv, v_st)]
    y_st = [x[n2:] + _dot(m.astype(BF16), z) for x, m, z in zip(ars, m_r, pvv)]
    yield
    s_new = [(s + _dot(x, y, _TN)) * w[CHUNK - 1:CHUNK, :] for s, x, y, w in zip(s_old, pvv, bk, w_incl)]
    for (b, p), s in zip(probs, s_new):
        s_ref[b, p] = s
    yield
    for (b, p), ys, rr, kk2, vv in zip(probs, y_st, r, k2, v):
        cols = slice(p * LANES, (p + 1) * LANES)
        y = ys[:CHUNK, :] + ys[CHUNK:, :]
        yc = y - _pair_sum(y, lo_mask) * (1.0 / RWKV_N)
        var = _pair_sum(yc * yc, lo_mask) * (1.0 / RWKV_N)
        y = yc * lax.rsqrt(var + RWKV_LN_EPS) * prm[5:6, cols] + prm[6:7, cols]
        y = y + _pair_sum(rr * kk2 * prm[4:5, cols], lo_mask) * vv
        o_ref[b, :, OUT_RWKV + p * LANES:OUT_RWKV + (p + 1) * LANES] = (y * g_all[_brows(b), cols]).astype(BF16)


OUTPROJ_TILES = 4


def _outproj_steps(x_ref, mix_ref, w_ref, o_ref):
    nb = x_ref.shape[0]
    mix = mix_ref[...].reshape(nb * CHUNK, D_MODEL)
    tn = D_MODEL // OUTPROJ_TILES
    for j in range(OUTPROJ_TILES):
        cols = slice(j * tn, (j + 1) * tn)
        o_ref[:, :, cols] = x_ref[:, :, cols] + _dot(mix, w_ref[:, cols]).reshape(nb, CHUNK, tn)
        yield


def _mixers_kernel(u_ref, x_ref, wout_ref, gdn_cw, gdn_prm, ml_cw, ml_prm, rg_cw, rg_prm, rg_wa, rg_wx,
                   rw_mu, rw_prm, rw_wlo, rw_g2, o_ref,
                   mix, gdn_buf, gdn_s, ml_buf, ml_c, ml_n, ml_m, rg_buf, rg_h, rw_buf, rw_s):
    @pl.when(pl.program_id(0) == 0)
    def _():
        for buf in (gdn_buf, ml_buf, rg_buf, rw_buf):
            buf[:, 0:HALO, :] = jnp.zeros((buf.shape[0], HALO, buf.shape[2]), F32)
        for state in (mix, gdn_s, ml_c, ml_n, ml_m, rg_h, rw_s):
            state[...] = jnp.zeros_like(state)

    stages = [
        _outproj_steps(x_ref, mix, wout_ref, o_ref),
        _rwkv_steps(u_ref, rw_mu, rw_prm, rw_wlo, rw_g2, mix, rw_buf, rw_s),
        _gdn_steps(u_ref, gdn_cw, gdn_prm, mix, gdn_buf, gdn_s),
        _mlstm_steps(u_ref, ml_cw, ml_prm, mix, ml_buf, ml_c, ml_n, ml_m),
        _rglru_steps(u_ref, rg_cw, rg_prm, rg_wa, rg_wx, mix, rg_buf, rg_h),
    ]
    n_rounds = [0] * len(stages)
    for done in itertools.zip_longest(*stages, fillvalue="done"):
        for i, d in enumerate(done):
            n_rounds[i] += d != "done"
    assert all(n > n_rounds[0] for n in n_rounds[1:]), n_rounds


def _mixers(u, x, w_out, layer, gdn_cw, gdn_prm, ml_cw, ml_prm, rg_cw, rg_prm, rg_wa, rg_wx,
            rw_mu, rw_prm, rw_wlo, rw_g2):
    bsz, seq, _ = u.shape
    n_chunks = seq // CHUNK

    def whole(a):
        return pl.BlockSpec(a.shape, lambda t, _n=a.ndim: (0,) * _n)

    def prev_chunk(t):
        return (0, jnp.maximum(t - 1, 0), 0)

    params = (gdn_cw, gdn_prm, ml_cw, ml_prm, rg_cw, rg_prm, rg_wa, rg_wx, rw_mu, rw_prm, rw_wlo, rw_g2)
    return pl.pallas_call(
        _mixers_kernel,
        out_shape=jax.ShapeDtypeStruct((bsz, seq, D_MODEL), F32),
        grid=(n_chunks + 1,),
        in_specs=[pl.BlockSpec((bsz, CHUNK, U_COLS), lambda t: (0, jnp.minimum(t, n_chunks - 1), 0)),
                  pl.BlockSpec((bsz, CHUNK, D_MODEL), prev_chunk),
                  pl.BlockSpec((None, D_MODEL, D_MODEL), lambda t: (layer, 0, 0))] + [whole(a) for a in params],
        out_specs=pl.BlockSpec((bsz, CHUNK, D_MODEL), prev_chunk),
        scratch_shapes=[
            pltpu.VMEM((bsz, CHUNK, D_MODEL), BF16),
            pltpu.VMEM((bsz, HALO + CHUNK, 3 * GROUP_W), F32),
            pltpu.VMEM((bsz, N_HEADS, HEAD_D, HEAD_D), F32),
            pltpu.VMEM((bsz, HALO + CHUNK, 2 * GROUP_W), F32),
            pltpu.VMEM((bsz, N_HEADS, HEAD_D, HEAD_D), F32),
            pltpu.VMEM((bsz, N_HEADS, 8, HEAD_D), F32),
            pltpu.VMEM((bsz, N_HEADS, 8, HEAD_D), F32),
            pltpu.VMEM((bsz, HALO + CHUNK, GROUP_W), F32),
            pltpu.VMEM((bsz, 8, GROUP_W), F32),
            pltpu.VMEM((bsz, HALO + CHUNK, RWKV_USED), F32),
            pltpu.VMEM((bsz, N_PAIRS, LANES, LANES), F32),
        ],
        compiler_params=pltpu.CompilerParams(
            dimension_semantics=("arbitrary",), vmem_limit_bytes=VMEM_LIMIT),
        name="mixers",
    )(u, x, w_out, *params)


def _ffn_kernel(x_ref, nw_ref, wg_ref, wu_ref, wd_ref, fw_ref, o_ref, n_scr, *, n_f, final):
    f = pl.program_id(1)

    @pl.when(f == 0)
    def _():
        x = x_ref[...]
        n_scr[...] = _rms(x, nw_ref[...]).astype(BF16)
        o_ref[...] = x

    n = n_scr[...]
    gate = jnp.dot(n, wg_ref[...], preferred_element_type=F32)
    up = jnp.dot(n, wu_ref[...], preferred_element_type=F32)
    hid = (_silu(gate) * up).astype(BF16)
    o_ref[...] += jnp.dot(hid, wd_ref[...], preferred_element_type=F32)

    if final:
        @pl.when(f == n_f - 1)
        def _():
            o_ref[...] = _rms(o_ref[...], fw_ref[...])


def _ffn(x2d, norm_w, w_gate, w_up, w_down, final_w, layer, *, final, tm=1024, tf=512):
    n_tok = x2d.shape[0]
    n_f = FFN_HIDDEN // tf
    return pl.pallas_call(
        functools.partial(_ffn_kernel, n_f=n_f, final=final),
        out_shape=jax.ShapeDtypeStruct((n_tok, D_MODEL), F32),
        grid=(n_tok // tm, n_f),
        in_specs=[
            pl.BlockSpec((tm, D_MODEL), lambda i, f: (i, 0)),
            pl.BlockSpec((1, D_MODEL), lambda i, f: (0, 0)),
            pl.BlockSpec((None, D_MODEL, tf), lambda i, f: (layer, 0, f)),
            pl.BlockSpec((None, D_MODEL, tf), lambda i, f: (layer, 0, f)),
            pl.BlockSpec((None, tf, D_MODEL), lambda i, f: (layer, f, 0)),
            pl.BlockSpec((1, D_MODEL), lambda i, f: (0, 0)),
        ],
        out_specs=pl.BlockSpec((tm, D_MODEL), lambda i, f: (i, 0)),
        scratch_shapes=[pltpu.VMEM((tm, D_MODEL), BF16)],
        compiler_params=pltpu.CompilerParams(
            dimension_semantics=("parallel", "arbitrary"), vmem_limit_bytes=VMEM_LIMIT),
        name="ffn_final" if final else "ffn",
    )(x2d, norm_w.reshape(1, D_MODEL), w_gate, w_up, w_down, final_w.reshape(1, D_MODEL))


def _pad_lanes(v, offset, width):
    v = v.astype(F32)
    return jnp.concatenate([jnp.zeros((offset,), F32), v, jnp.zeros((width - offset - v.shape[0],), F32)])


def _rows(rows, width):
    rows = [r.astype(F32).reshape(width) for r in rows]
    return jnp.stack(rows + [jnp.zeros((width,), F32)] * (8 - len(rows)))


def _perm_w_in(w):
    wt = jnp.swapaxes(w, 1, 2).astype(BF16)
    pad = jnp.zeros((w.shape[0], RGLRU_C0 - GATES_C0 - 16, w.shape[1]), BF16)
    return jnp.concatenate([wt[:, 0:2048], wt[:, 2056:4104], wt[:, 5136:6928], wt[:, 2048:2056], wt[:, 4104:4112],
                            pad, wt[:, 4112:5136]], axis=1)


def kernel(x, attn_norm, w_in, gdn_conv, gdn_a_log, gdn_dt_bias, gdn_norm, mlstm_conv, mlstm_b_i, mlstm_b_f, mlstm_norm, rglru_conv, rglru_conv_b, rglru_w_a, rglru_b_a, rglru_w_x, rglru_b_x, rglru_lambda, rwkv_mu, rwkv_w0, rwkv_w2, rwkv_a0, rwkv_a2, rwkv_g2, rwkv_k_k, rwkv_k_a, rwkv_r_k, rwkv_ln_w, rwkv_ln_b, w_out, ffn_norm, ffn_w_gate, ffn_w_up, ffn_w_down, final_norm):
    bsz, seq, d = x.shape
    depth = w_in.shape[0]
    x2d = x.reshape(bsz * seq, d)
    w_perm = _perm_w_in(w_in)
    w_out_bf, w_gate_bf, w_up_bf, w_down_bf = [w.astype(BF16) for w in (w_out, ffn_w_gate, ffn_w_up, ffn_w_down)]
    for l in range(depth):
        u = _inproj(x2d, attn_norm[l], w_perm, l).reshape(bsz, seq, U_COLS)

        gdn_prm = _rows([_pad_lanes(gdn_a_log[l], 0, LANES), _pad_lanes(gdn_dt_bias[l], 0, LANES),
                         gdn_norm[l]], LANES)
        ml_bias = _pad_lanes(mlstm_b_i[l], 2 * N_HEADS, LANES) + _pad_lanes(mlstm_b_f[l], 3 * N_HEADS, LANES)
        ml_prm = _rows([ml_bias] + [mlstm_norm[l, h] for h in range(N_HEADS)], LANES)
        rg_prm = _rows([rglru_conv_b[l], rglru_b_a[l], rglru_b_x[l], rglru_lambda[l]], GROUP_W)
        rw_prm = _rows([rwkv_w0[l], rwkv_a0[l], rwkv_k_k[l], rwkv_k_a[l], rwkv_r_k[l],
                        rwkv_ln_w[l], rwkv_ln_b[l]], GROUP_W)
        zeros_lora = jnp.zeros((RWKV_N, GROUP_W), F32)
        w_lora = jnp.concatenate([
            jnp.concatenate([rwkv_w2[l], zeros_lora], axis=1),
            jnp.concatenate([zeros_lora, rwkv_a2[l]], axis=1)], axis=0).astype(BF16)
        x2d = _mixers(u, x2d.reshape(bsz, seq, d), w_out_bf, l,
                      gdn_conv[l], gdn_prm, mlstm_conv[l], ml_prm, rglru_conv[l], rg_prm,
                      rglru_w_a[l].astype(BF16), rglru_w_x[l].astype(BF16),
                      rwkv_mu[l].reshape(1, RWKV_USED), rw_prm, w_lora, rwkv_g2[l].astype(BF16)
                      ).reshape(bsz * seq, d)
        x2d = _ffn(x2d, ffn_norm[l], w_gate_bf, w_up_bf, w_down_bf, final_norm, l, final=(l == depth - 1))
    return x2d.reshape(bsz, seq, d)
```

```python
import functools
import itertools

import jax
import jax.numpy as jnp
from jax import lax
from jax.experimental import pallas as pl
from jax.experimental.pallas import tpu as pltpu

F32 = jnp.float32
BF16 = jnp.bfloat16

D_MODEL = 2048
GROUP_W = 512
HEAD_D = 128
N_HEADS = 4
RWKV_N = 64
N_PAIRS = GROUP_W // (2 * RWKV_N)
CONV_K = 4
CHUNK = 64
FFN_HIDDEN = 5632
NORM_EPS = 1e-6
RWKV_LN_EPS = 64e-5
RGLRU_C = 8.0
LANES = 128
HALO = 8
INV_LEVELS = 5

U_COLS = 7168
GDN_C0 = 0
MLSTM_C0 = 2048
RWKV_C0 = 4096
RWKV_USED = 1792
GATES_C0 = 5888
RGLRU_C0 = 6144
OUT_GDN, OUT_MLSTM, OUT_RGLRU, OUT_RWKV = 0, GROUP_W, 2 * GROUP_W, 3 * GROUP_W

VMEM_LIMIT = 60 * 1024 * 1024

_NN = (((1,), (0,)), ((), ()))
_NT = (((1,), (1,)), ((), ()))
_TN = (((0,), (0,)), ((), ()))


def _dot(a, b, dims=_NN):
    return lax.dot_general(a, b, dims, preferred_element_type=F32)


def _mm(a, b, dims=_NN):
    return _dot(a.astype(BF16), b.astype(BF16), dims)


def _sigmoid(x):
    return 1.0 / (1.0 + jnp.exp(-x))


def _softplus(x):
    return jnp.maximum(x, 0.0) + jnp.log(1.0 + jnp.exp(-jnp.abs(x)))


def _silu(x):
    return x * _sigmoid(x)


def _rms(x, w, eps=NORM_EPS):
    return x * lax.rsqrt(jnp.mean(x * x, axis=-1, keepdims=True) + eps) * w


def _chunk_cumsum(x):
    row = lax.broadcasted_iota(jnp.int32, x.shape, 0) % CHUNK
    d = 1
    while d < CHUNK:
        x = x + jnp.where(row >= d, pltpu.roll(x, d, 0), 0.0)
        d *= 2
    return x


def _chunk_cumsum_mxu(x):
    ri = lax.broadcasted_iota(jnp.int32, (CHUNK, CHUNK), 0)
    ci = lax.broadcasted_iota(jnp.int32, (CHUNK, CHUNK), 1)
    tri = (ri >= ci).astype(BF16)
    hi = x.astype(BF16)
    r1 = x - hi.astype(F32)
    mid = r1.astype(BF16)
    lo = (r1 - mid.astype(F32)).astype(BF16)
    outs = []
    for c in range(x.shape[0] // CHUNK):
        rows = _brows(c)
        outs.append(_dot(tri, hi[rows, :]) + (_dot(tri, mid[rows, :]) + _dot(tri, lo[rows, :])))
    return jnp.concatenate(outs, axis=0)


def _split(a):
    hi = a.astype(BF16)
    return hi, (a - hi.astype(F32)).astype(BF16)


def _mm3(a, b):
    return _dot(a[0], b[0]) + (_dot(a[0], b[1]) + _dot(a[1], b[0]))


def _inv_unit_lower_steps(a_list, out):
    n = a_list[0].shape[0]
    eye = (lax.broadcasted_iota(jnp.int32, (n, n), 0) == lax.broadcasted_iota(jnp.int32, (n, n), 1)).astype(F32)
    inv = [eye - a for a in a_list]
    p = [(-a).astype(BF16) for a in a_list]
    for _ in range(INV_LEVELS):
        p = [_dot(x, x).astype(BF16) for x in p]
        yield
        inv = [x + _dot(x.astype(BF16), y) for x, y in zip(inv, p)]
        yield
    a_s = [_split(a) for a in a_list]
    t_s = [_split(t) for t in inv]
    a_t = [_mm3(x, y) for x, y in zip(a_s, t_s)]
    yield
    out[:] = [t + _dot(ts[0], (eye - t - x).astype(BF16)) for t, ts, x in zip(inv, t_s, a_t)]


def _causal_conv(buf, x, conv_w):
    tt = x.shape[0]
    buf[HALO:HALO + tt, :] = x
    xx = buf[...]
    acc = conv_w[0:1, :] * xx
    for j in range(1, CONV_K):
        acc = conv_w[j:j + 1, :] * xx + pltpu.roll(acc, 1, 0)
    buf[0:HALO, :] = x[tt - HALO:tt, :]
    return acc[HALO:, :]


def _brows(b):
    return slice(b * CHUNK, (b + 1) * CHUNK)


def _hcols(h, base=0):
    return slice(base + h * HEAD_D, base + (h + 1) * HEAD_D)


def _inproj_kernel(x_ref, nw_ref, w_ref, *refs, n_cast):
    cast_in, o_ref, cast_out, n_scr = refs[:n_cast], refs[n_cast], refs[n_cast + 1:2 * n_cast + 1], refs[-1]

    @pl.when(pl.program_id(1) == 0)
    def _():
        n_scr[...] = _rms(x_ref[...], nw_ref[...]).astype(BF16)

    o_ref[...] = _dot(n_scr[...], w_ref[...], _NT)
    for src, dst in zip(cast_in, cast_out):
        dst[...] = src[...].astype(BF16)


def _cast_block_rows(rows, n_steps):
    rb = 16
    while rows % rb or rows // rb > n_steps:
        rb += 16
    return rb


def _inproj(x2d, norm_w, w_perm, layer, cast_ws, *, tm=1024, tn=1792):
    n_tok = x2d.shape[0]
    n_i, n_j = n_tok // tm, U_COLS // tn
    in_specs = [
        pl.BlockSpec((tm, D_MODEL), lambda i, j: (i, 0)),
        pl.BlockSpec((1, D_MODEL), lambda i, j: (0, 0)),
        pl.BlockSpec((None, tn, D_MODEL), lambda i, j: (layer, j, 0)),
    ]
    out_specs = [pl.BlockSpec((tm, tn), lambda i, j: (i, j))]
    out_shape = [jax.ShapeDtypeStruct((n_tok, U_COLS), F32)]
    for w in cast_ws:
        _, rows, cols = w.shape
        rb = _cast_block_rows(rows, n_i * n_j)

        def blk(i, j, _last=rows // rb - 1):
            return jnp.minimum(i * n_j + j, _last)

        in_specs.append(pl.BlockSpec((None, rb, cols), lambda i, j, _b=blk: (layer, _b(i, j), 0)))
        out_specs.append(pl.BlockSpec((None, rb, cols), lambda i, j, _b=blk: (0, _b(i, j), 0)))
        out_shape.append(jax.ShapeDtypeStruct((1, rows, cols), BF16))
    outs = pl.pallas_call(
        functools.partial(_inproj_kernel, n_cast=len(cast_ws)),
        out_shape=out_shape,
        grid=(n_i, n_j),
        in_specs=in_specs,
        out_specs=out_specs,
        scratch_shapes=[pltpu.VMEM((tm, D_MODEL), BF16)],
        compiler_params=pltpu.CompilerParams(
            dimension_semantics=("arbitrary", "arbitrary"), vmem_limit_bytes=VMEM_LIMIT),
        name="inproj",
    )(x2d, norm_w.reshape(1, D_MODEL), w_perm, *cast_ws)
    return outs[0], outs[1:]


def _gdn_steps(u_ref, cw_ref, prm_ref, o_ref, buf, s_ref):
    nb = u_ref.shape[0]
    cw = cw_ref[...]
    qkv = _silu(jnp.concatenate(
        [_causal_conv(buf.at[b], u_ref[b, :, GDN_C0:GDN_C0 + 3 * GROUP_W], cw) for b in range(nb)], axis=0))
    yield
    gates = u_ref[:, :, GATES_C0:GATES_C0 + LANES].reshape(nb * CHUNK, LANES)
    prm = prm_ref[...]
    g_all = -jnp.exp(prm[0:1, :]) * _softplus(gates + prm[1:2, :])
    beta_all = _sigmoid(gates)
    gc_all = _chunk_cumsum(g_all)
    norm_w = prm[2:3, :]
    gc_t = [gc_all[_brows(b), :].T for b in range(nb)]
    yield

    qn, kn = [], []
    for h in range(N_HEADS):
        q = qkv[:, _hcols(h)]
        k = qkv[:, _hcols(h, GROUP_W)]
        qn.append(q * (lax.rsqrt(jnp.sum(q * q, axis=-1, keepdims=True) + 1e-6) * (HEAD_D ** -0.5)))
        kn.append(k * lax.rsqrt(jnp.sum(k * k, axis=-1, keepdims=True) + 1e-6))
    yield

    ri = lax.broadcasted_iota(jnp.int32, (CHUNK, CHUNK), 0)
    ci = lax.broadcasted_iota(jnp.int32, (CHUNK, CHUNK), 1)
    causal = ri >= ci
    strict = ri > ci

    probs = [(b, h) for b in range(nb) for h in range(N_HEADS)]
    q = [qn[h][_brows(b), :] for b, h in probs]
    k = [kn[h][_brows(b), :] for b, h in probs]
    v = [qkv[_brows(b), _hcols(h, 2 * GROUP_W)] for b, h in probs]
    beta = [beta_all[_brows(b), N_HEADS + h:N_HEADS + h + 1] for b, h in probs]
    gcol = [gc_all[_brows(b), h:h + 1] for b, h in probs]
    grow = [gc_t[b][h:h + 1, :] for b, h in probs]
    decay = [jnp.where(causal, jnp.exp(jnp.where(causal, gc - gr, 0.0)), 0.0) for gc, gr in zip(gcol, grow)]
    yield
    kb = [x * y for x, y in zip(k, beta)]
    raw = [_mm(jnp.concatenate([x, y], axis=0), z, _NT) for x, y, z in zip(kb, q, k)]
    yield
    a_mat = [jnp.where(strict, r[:CHUNK] * d, 0.0) for r, d in zip(raw, decay)]
    attn = [r[CHUNK:] * d for r, d in zip(raw, decay)]
    t_mat = []
    yield from _inv_unit_lower_steps(a_mat, t_mat)
    eg = [jnp.exp(x) for x in gcol]
    sol = [_mm(t, jnp.concatenate([x * y, z * e], axis=1))
           for t, x, y, z, e in zip(t_mat, v, beta, kb, eg)]
    yield
    s_old = [s_ref[b, h] for b, h in probs]
    ws = [_mm(jnp.concatenate([x[:, HEAD_D:], y * e], axis=0), s)
          for x, y, e, s in zip(sol, q, eg, s_old)]
    yield
    v_new = [x[:, :HEAD_D] - y[:CHUNK] for x, y in zip(sol, ws)]
    o = [y[CHUNK:] + _mm(a, x) for y, a, x in zip(ws, attn, v_new)]
    yield
    glast = [x[CHUNK - 1:CHUNK, :] for x in gcol]
    s_new = [s * jnp.exp(gl) + _mm(x * jnp.exp(gl - gc), vn, _TN)
             for s, gl, x, gc, vn in zip(s_old, glast, k, gcol, v_new)]
    for (b, h), s in zip(probs, s_new):
        s_ref[b, h] = s
    yield
    for (b, h), x in zip(probs, o):
        z = u_ref[b, :, _hcols(h, GDN_C0 + 3 * GROUP_W)]
        o_ref[b, :, _hcols(h, OUT_GDN)] = (_rms(x, norm_w) * _silu(z)).astype(BF16)


def _mlstm_steps(u_ref, cw_ref, prm_ref, o_ref, buf, c_ref, n_ref, m_ref):
    nb = u_ref.shape[0]
    cw = cw_ref[...]
    qk = _silu(jnp.concatenate(
        [_causal_conv(buf.at[b], u_ref[b, :, MLSTM_C0:MLSTM_C0 + 2 * GROUP_W], cw) for b in range(nb)], axis=0))
    yield
    gates = u_ref[:, :, GATES_C0:GATES_C0 + LANES].reshape(nb * CHUNK, LANES)
    prm = prm_ref[...]
    pre = gates + prm[0:1, :]
    b_all = _chunk_cumsum(-_softplus(-pre))
    b_tr = [b_all[_brows(b), :].T for b in range(nb)]
    i_tr = [pre[_brows(b), :].T for b in range(nb)]
    yield

    ri = lax.broadcasted_iota(jnp.int32, (CHUNK, CHUNK), 0)
    ci = lax.broadcasted_iota(jnp.int32, (CHUNK, CHUNK), 1)
    causal = ri >= ci

    probs = [(b, h) for b in range(nb) for h in range(N_HEADS)]
    li = [2 * N_HEADS + h for _, h in probs]
    lf = [3 * N_HEADS + h for _, h in probs]
    q = [qk[_brows(b), _hcols(h)] for b, h in probs]
    k = [qk[_brows(b), _hcols(h, GROUP_W)] * (HEAD_D ** -0.5) for b, h in probs]
    v = [u_ref[b, :, _hcols(h, MLSTM_C0 + 2 * GROUP_W)] for b, h in probs]
    bcol = [b_all[_brows(b), l:l + 1] for (b, _), l in zip(probs, lf)]
    icol = [pre[_brows(b), l:l + 1] for (b, _), l in zip(probs, li)]
    brow = [b_tr[b][l:l + 1, :] for (b, _), l in zip(probs, lf)]
    irow = [i_tr[b][l:l + 1, :] for (b, _), l in zip(probs, li)]
    m_old = [m_ref[b, h][0:1, 0:1] for b, h in probs]
    c_old = [c_ref[b, h] for b, h in probs]
    n_old = [n_ref[b, h][0:1, :] for b, h in probs]
    yield

    d = [jnp.where(causal, bc - br + ir, -jnp.inf) for bc, br, ir in zip(bcol, brow, irow)]
    inter = [bc + m for bc, m in zip(bcol, m_old)]
    m_t = [jnp.maximum(x, jnp.max(y, axis=-1, keepdims=True)) for x, y in zip(inter, d)]
    yield
    qk_raw = [_mm(x, y, _NT) for x, y in zip(q, k)]
    qc = [_mm(x, c) for x, c in zip(q, c_old)]
    yield
    s = [r * jnp.exp(x - m) for r, x, m in zip(qk_raw, d, m_t)]
    a = [jnp.exp(x - m) for x, m in zip(inter, m_t)]
    yield
    sv = [_mm(x, y) for x, y in zip(s, v)]
    yield
    num = [x * y + z for x, y, z in zip(a, qc, sv)]
    den = [x * jnp.sum(y * n, axis=-1, keepdims=True) + jnp.sum(z, axis=-1, keepdims=True)
           for x, y, n, z in zip(a, q, n_old, s)]
    h_t = [x / jnp.maximum(jnp.abs(y), jnp.exp(-m)) for x, y, m in zip(num, den, m_t)]
    yield
    g = [x[CHUNK - 1:CHUNK, :] for x in bcol]
    w_log = [x - bc + ic for x, bc, ic in zip(g, bcol, icol)]
    m_new = [jnp.maximum(x + m, jnp.max(w, axis=0, keepdims=True)) for x, m, w in zip(g, m_old, w_log)]
    scale = [jnp.exp(x + m - mn) for x, m, mn in zip(g, m_old, m_new)]
    kw = [x * jnp.exp(w - mn) for x, w, mn in zip(k, w_log, m_new)]
    yield
    c_new = [sc * c + _mm(x, y, _TN) for sc, c, x, y in zip(scale, c_old, kw, v)]
    n_new = [sc * n + jnp.sum(x, axis=0, keepdims=True) for sc, n, x in zip(scale, n_old, kw)]
    for (b, h), c, n, m in zip(probs, c_new, n_new, m_new):
        c_ref[b, h] = c
        n_ref[b, h] = jnp.broadcast_to(n, (8, HEAD_D))
        m_ref[b, h] = jnp.broadcast_to(m, (8, HEAD_D))
    yield
    for (b, h), x in zip(probs, h_t):
        o_gate = u_ref[b, :, _hcols(h, MLSTM_C0 + 3 * GROUP_W)]
        o_ref[b, :, _hcols(h, OUT_MLSTM)] = _rms(_sigmoid(o_gate) * x, prm[1 + h:2 + h, :]).astype(BF16)


def _rglru_steps(u_ref, cw_ref, prm_ref, wa_ref, wx_ref, o_ref, buf, h_ref):
    nb = u_ref.shape[0]
    prm = prm_ref[...]
    cw = cw_ref[...]
    xb = jnp.concatenate(
        [_causal_conv(buf.at[b], u_ref[b, :, RGLRU_C0:RGLRU_C0 + GROUP_W], cw) for b in range(nb)], axis=0)
    xb = xb + prm[0:1, :]
    yield
    ra = []
    rx = []
    for n in range(N_HEADS):
        blk = xb[:, _hcols(n)].astype(BF16)
        ra.append(_dot(blk, wa_ref[n]))
        rx.append(_dot(blk, wx_ref[n]))
    yield
    r = _sigmoid(jnp.concatenate(ra, axis=1) + prm[1:2, :])
    i = _sigmoid(jnp.concatenate(rx, axis=1) + prm[2:3, :])
    log_a = -RGLRU_C * r * _softplus(-prm[3:4, :])
    a = jnp.exp(log_a)
    th = jnp.tanh(log_a)
    bb = jnp.sqrt(-2.0 * th / (1.0 - th)) * (i * xb)
    yield
    row = lax.broadcasted_iota(jnp.int32, a.shape, 0) % HALO
    d = 1
    while d < HALO:
        keep = row >= d
        bb = jnp.where(keep, a * pltpu.roll(bb, d, 0) + bb, bb)
        a = jnp.where(keep, a * pltpu.roll(a, d, 0), a)
        d *= 2
        yield
    carry = [h_ref[b][0:1, :] for b in range(nb)]
    groups = [[] for _ in range(nb)]
    for g in range(CHUNK // HALO):
        for b in range(nb):
            lo = b * CHUNK + g * HALO
            hg = bb[lo:lo + HALO, :] + a[lo:lo + HALO, :] * carry[b]
            groups[b].append(hg)
            carry[b] = hg[HALO - 1:HALO, :]
    yield
    for b in range(nb):
        h_ref[b] = jnp.broadcast_to(carry[b], (8, GROUP_W))
        gate = u_ref[b, :, RGLRU_C0 + GROUP_W:RGLRU_C0 + 2 * GROUP_W]
        gelu = 0.5 * gate * (1.0 + jnp.tanh(0.7978845608028654 * (gate + 0.044715 * gate * gate * gate)))
        o_ref[b, :, OUT_RGLRU:OUT_RGLRU + GROUP_W] = (jnp.concatenate(groups[b], axis=0) * gelu).astype(BF16)


def _pair_sum(x, lo_mask):
    s_lo = jnp.sum(jnp.where(lo_mask, x, 0.0), axis=-1, keepdims=True)
    s_hi = jnp.sum(jnp.where(lo_mask, 0.0, x), axis=-1, keepdims=True)
    return jnp.where(lo_mask, s_lo, s_hi)


def _stack_heads(x, lo_mask):
    return jnp.concatenate([jnp.where(lo_mask, x, 0.0), jnp.where(lo_mask, 0.0, x)], axis=0)


def _rwkv_steps(u_ref, mu_ref, prm_ref, wlo_ref, g2_ref, o_ref, buf, s_ref):
    nb = u_ref.shape[0]
    mu = mu_ref[...]
    xs = []
    for b in range(nb):
        x = u_ref[b, :, RWKV_C0:RWKV_C0 + RWKV_USED]
        buf[b, HALO:HALO + CHUNK, :] = x
        x_prev = buf[b, HALO - 1:HALO - 1 + CHUNK, :]
        buf[b, 0:HALO, :] = x[CHUNK - HALO:CHUNK, :]
        xs.append(x + (x_prev - x) * mu)
    xs = jnp.concatenate(xs, axis=0)
    yield
    prm = prm_ref[...]
    r_all = xs[:, 0:GROUP_W]
    k_all = xs[:, GROUP_W:2 * GROUP_W]
    v_all = xs[:, 2 * GROUP_W:3 * GROUP_W]
    lora = xs[:, 3 * GROUP_W:3 * GROUP_W + LANES]
    lane = lax.broadcasted_iota(jnp.int32, lora.shape, 1)
    lora = jnp.where(lane < RWKV_N, jnp.tanh(lora), lora)
    wa = _mm(lora, wlo_ref[...])
    g_all = _mm(_sigmoid(xs[:, 3 * GROUP_W + LANES:3 * GROUP_W + 2 * LANES]), g2_ref[...])
    yield
    w_raw = -_softplus(-(prm[0:1, :] + wa[:, :GROUP_W])) - 0.5
    lw_all = -jnp.exp(w_raw)
    a_all = _sigmoid(prm[1:2, :] + wa[:, GROUP_W:])
    kk_all = k_all * prm[2:3, :]
    k2_all = k_all * (1.0 + (a_all - 1.0) * prm[3:4, :])
    yield
    lc_all = _chunk_cumsum_mxu(lw_all)
    yield

    lo_mask = lax.broadcasted_iota(jnp.int32, (CHUNK, LANES), 1) < RWKV_N
    n2 = 2 * CHUNK
    ri = lax.broadcasted_iota(jnp.int32, (n2, n2), 0)
    ci = lax.broadcasted_iota(jnp.int32, (n2, n2), 1)
    same = (ri // CHUNK) == (ci // CHUNK)
    strict = same & ((ci % CHUNK) < (ri % CHUNK))
    incl = same & ((ci % CHUNK) <= (ri % CHUNK))

    probs = [(b, p) for b in range(nb) for p in range(N_PAIRS)]

    def sel(arr):
        return [arr[_brows(b), p * LANES:(p + 1) * LANES] for b, p in probs]

    r, v, k2, a_sig, lw, lc = sel(r_all), sel(v_all), sel(k2_all), sel(a_all), sel(lw_all), sel(lc_all)
    kk = [x * lax.rsqrt(_pair_sum(x * x, lo_mask) + 1e-6) for x in sel(kk_all)]
    yield
    w_incl = [jnp.exp(x) for x in lc]
    w_inv = [jnp.exp(-x) for x in lc]
    a_hat = [_stack_heads(-x * jnp.exp(c - w), lo_mask) for x, c, w in zip(kk, lc, lw)]
    b_hat = [_stack_heads(x * y * w, lo_mask) for x, y, w in zip(kk, a_sig, w_inv)]
    yield
    k_hat = [_stack_heads(x * w, lo_mask) for x, w in zip(k2, w_inv)]
    r_hat = [_stack_heads(x * w, lo_mask) for x, w in zip(r, w_incl)]
    v_st = [_stack_heads(x, lo_mask) for x in v]
    s_old = [s_ref[b, p] for b, p in probs]
    yield

    ar = [jnp.concatenate([x, y], axis=0).astype(BF16) for x, y in zip(a_hat, r_hat)]
    bk = [jnp.concatenate([x, y], axis=0).astype(BF16) for x, y in zip(b_hat, k_hat)]
    cross = [_dot(x, y, _NT) for x, y in zip(ar, bk)]
    yield
    a_ab = [jnp.where(strict, x[:n2, :n2], 0.0) for x in cross]
    a_ak = [jnp.where(strict, x[:n2, n2:], 0.0) for x in cross]
    m_r = [jnp.concatenate([jnp.where(incl, x[n2:, :n2], 0.0), jnp.where(incl, x[n2:, n2:], 0.0)], axis=1)
           for x in cross]
    yield
    ars = [_dot(x, s.astype(BF16), _NT) for x, s in zip(ar, s_old)]
    akv = [_mm(y, z) for y, z in zip(a_ak, v_st)]
    t_mat = []
    yield from _inv_unit_lower_steps([-x for x in a_ab], t_mat)
    pv = [_mm(t, x[:n2] + y) for t, x, y in zip(t_mat, ars, akv)]
    yield
    pvv = [jnp.concatenate([x, y], axis=0).astype(BF16) for x, y in zip(pv, v_st)]
    y_st = [x[n2:] + _dot(m.astype(BF16), z) for x, m, z in zip(ars, m_r, pvv)]
    yield
    s_new = [(s + _dot(x, y, _TN)) * w[CHUNK - 1:CHUNK, :] for s, x, y, w in zip(s_old, pvv, bk, w_incl)]
    for (b, p), s in zip(probs, s_new):
        s_ref[b, p] = s
    yield
    for (b, p), ys, rr, kk2, vv in zip(probs, y_st, r, k2, v):
        cols = slice(p * LANES, (p + 1) * LANES)
        y = ys[:CHUNK, :] + ys[CHUNK:, :]
        yc = y - _pair_sum(y, lo_mask) * (1.0 / RWKV_N)
        var = _pair_sum(yc * yc, lo_mask) * (1.0 / RWKV_N)
        y = yc * lax.rsqrt(var + RWKV_LN_EPS) * prm[5:6, cols] + prm[6:7, cols]
        y = y + _pair_sum(rr * kk2 * prm[4:5, cols], lo_mask) * vv
        o_ref[b, :, OUT_RWKV + p * LANES:OUT_RWKV + (p + 1) * LANES] = (y * g_all[_brows(b), cols]).astype(BF16)


OUTPROJ_TILES = 4


def _outproj_steps(x_ref, mix_ref, w_ref, o_ref):
    nb = x_ref.shape[0]
    mix = mix_ref[...].reshape(nb * CHUNK, D_MODEL)
    tn = D_MODEL // OUTPROJ_TILES
    for j in range(OUTPROJ_TILES):
        cols = slice(j * tn, (j + 1) * tn)
        o_ref[:, :, cols] = x_ref[:, :, cols] + _dot(mix, w_ref[:, cols]).reshape(nb, CHUNK, tn)
        yield


def _mixers_kernel(u_ref, x_ref, wout_ref, gdn_cw, gdn_prm, ml_cw, ml_prm, rg_cw, rg_prm, rg_wa, rg_wx,
                   rw_mu, rw_prm, rw_wlo, rw_g2, o_ref,
                   mix, gdn_buf, gdn_s, ml_buf, ml_c, ml_n, ml_m, rg_buf, rg_h, rw_buf, rw_s):
    @pl.when(pl.program_id(0) == 0)
    def _():
        for buf in (gdn_buf, ml_buf, rg_buf, rw_buf):
            buf[:, 0:HALO, :] = jnp.zeros((buf.shape[0], HALO, buf.shape[2]), F32)
        for state in (mix, gdn_s, ml_c, ml_n, ml_m, rg_h, rw_s):
            state[...] = jnp.zeros_like(state)

    last = pl.num_programs(0) - 1

    @pl.when(pl.program_id(0) < last)
    def _():
        stages = [
            _outproj_steps(x_ref, mix, wout_ref, o_ref),
            _rwkv_steps(u_ref, rw_mu, rw_prm, rw_wlo, rw_g2, mix, rw_buf, rw_s),
            _gdn_steps(u_ref, gdn_cw, gdn_prm, mix, gdn_buf, gdn_s),
            _mlstm_steps(u_ref, ml_cw, ml_prm, mix, ml_buf, ml_c, ml_n, ml_m),
            _rglru_steps(u_ref, rg_cw, rg_prm, rg_wa, rg_wx, mix, rg_buf, rg_h),
        ]
        n_rounds = [0] * len(stages)
        for done in itertools.zip_longest(*stages, fillvalue="done"):
            for i, d in enumerate(done):
                n_rounds[i] += d != "done"
        assert all(n > n_rounds[0] for n in n_rounds[1:]), n_rounds

    @pl.when(pl.program_id(0) == last)
    def _():
        for _ in _outproj_steps(x_ref, mix, wout_ref, o_ref):
            pass


def _mixers(u, x, w_out, layer, gdn_cw, gdn_prm, ml_cw, ml_prm, rg_cw, rg_prm, rg_wa, rg_wx,
            rw_mu, rw_prm, rw_wlo, rw_g2):
    bsz, seq, _ = u.shape
    n_chunks = seq // CHUNK

    def whole(a):
        return pl.BlockSpec(a.shape, lambda t, _n=a.ndim: (0,) * _n)

    def prev_chunk(t):
        return (0, jnp.maximum(t - 1, 0), 0)

    params = (gdn_cw, gdn_prm, ml_cw, ml_prm, rg_cw, rg_prm, rg_wa, rg_wx, rw_mu, rw_prm, rw_wlo, rw_g2)
    return pl.pallas_call(
        _mixers_kernel,
        out_shape=jax.ShapeDtypeStruct((bsz, seq, D_MODEL), F32),
        grid=(n_chunks + 1,),
        in_specs=[pl.BlockSpec((bsz, CHUNK, U_COLS), lambda t: (0, jnp.minimum(t, n_chunks - 1), 0)),
                  pl.BlockSpec((bsz, CHUNK, D_MODEL), prev_chunk),
                  pl.BlockSpec((None, D_MODEL, D_MODEL), lambda t: (layer, 0, 0))] + [whole(a) for a in params],
        out_specs=pl.BlockSpec((bsz, CHUNK, D_MODEL), prev_chunk),
        scratch_shapes=[
            pltpu.VMEM((bsz, CHUNK, D_MODEL), BF16),
            pltpu.VMEM((bsz, HALO + CHUNK, 3 * GROUP_W), F32),
            pltpu.VMEM((bsz, N_HEADS, HEAD_D, HEAD_D), F32),
            pltpu.VMEM((bsz, HALO + CHUNK, 2 * GROUP_W), F32),
            pltpu.VMEM((bsz, N_HEADS, HEAD_D, HEAD_D), F32),
            pltpu.VMEM((bsz, N_HEADS, 8, HEAD_D), F32),
            pltpu.VMEM((bsz, N_HEADS, 8, HEAD_D), F32),
            pltpu.VMEM((bsz, HALO + CHUNK, GROUP_W), F32),
            pltpu.VMEM((bsz, 8, GROUP_W), F32),
            pltpu.VMEM((bsz, HALO + CHUNK, RWKV_USED), F32),
            pltpu.VMEM((bsz, N_PAIRS, LANES, LANES), F32),
        ],
        compiler_params=pltpu.CompilerParams(
            dimension_semantics=("arbitrary",), vmem_limit_bytes=VMEM_LIMIT),
        name="mixers",
    )(u, x, w_out, *params)


def _ffn_kernel(x_ref, nw_ref, wg_ref, wu_ref, wd_ref, fw_ref, o_ref, n_scr, *, n_f, final):
    f = pl.program_id(1)

    @pl.when(f == 0)
    def _():
        x = x_ref[...]
        n_scr[...] = _rms(x, nw_ref[...]).astype(BF16)
        o_ref[...] = x

    n = n_scr[...]
    gate = jnp.dot(n, wg_ref[...], preferred_element_type=F32)
    up = jnp.dot(n, wu_ref[...], preferred_element_type=F32)
    hid = (_silu(gate) * up).astype(BF16)
    o_ref[...] += jnp.dot(hid, wd_ref[...], preferred_element_type=F32)

    if final:
        @pl.when(f == n_f - 1)
        def _():
            o_ref[...] = _rms(o_ref[...], fw_ref[...])


def _ffn(x2d, norm_w, w_gate, w_up, w_down, final_w, layer, *, final, tm=1024, tf=512):
    n_tok = x2d.shape[0]
    n_f = FFN_HIDDEN // tf
    return pl.pallas_call(
        functools.partial(_ffn_kernel, n_f=n_f, final=final),
        out_shape=jax.ShapeDtypeStruct((n_tok, D_MODEL), F32),
        grid=(n_tok // tm, n_f),
        in_specs=[
            pl.BlockSpec((tm, D_MODEL), lambda i, f: (i, 0)),
            pl.BlockSpec((1, D_MODEL), lambda i, f: (0, 0)),
            pl.BlockSpec((None, D_MODEL, tf), lambda i, f: (layer, 0, f)),
            pl.BlockSpec((None, D_MODEL, tf), lambda i, f: (layer, 0, f)),
            pl.BlockSpec((None, tf, D_MODEL), lambda i, f: (layer, f, 0)),
            pl.BlockSpec((1, D_MODEL), lambda i, f: (0, 0)),
        ],
        out_specs=pl.BlockSpec((tm, D_MODEL), lambda i, f: (i, 0)),
        scratch_shapes=[pltpu.VMEM((tm, D_MODEL), BF16)],
        compiler_params=pltpu.CompilerParams(
            dimension_semantics=("parallel", "arbitrary"), vmem_limit_bytes=VMEM_LIMIT),
        name="ffn_final" if final else "ffn",
    )(x2d, norm_w.reshape(1, D_MODEL), w_gate, w_up, w_down, final_w.reshape(1, D_MODEL))


def _pad_lanes(v, offset, width):
    v = v.astype(F32)
    return jnp.concatenate([jnp.zeros((offset,), F32), v, jnp.zeros((width - offset - v.shape[0],), F32)])


def _rows(rows, width):
    rows = [r.astype(F32).reshape(width) for r in rows]
    return jnp.stack(rows + [jnp.zeros((width,), F32)] * (8 - len(rows)))


def _perm_w_in(w):
    wt = jnp.swapaxes(w, 1, 2).astype(BF16)
    pad = jnp.zeros((w.shape[0], RGLRU_C0 - GATES_C0 - 16, w.shape[1]), BF16)
    return jnp.concatenate([wt[:, 0:2048], wt[:, 2056:4104], wt[:, 5136:6928], wt[:, 2048:2056], wt[:, 4104:4112],
                            pad, wt[:, 4112:5136]], axis=1)


def kernel(x, attn_norm, w_in, gdn_conv, gdn_a_log, gdn_dt_bias, gdn_norm, mlstm_conv, mlstm_b_i, mlstm_b_f, mlstm_norm, rglru_conv, rglru_conv_b, rglru_w_a, rglru_b_a, rglru_w_x, rglru_b_x, rglru_lambda, rwkv_mu, rwkv_w0, rwkv_w2, rwkv_a0, rwkv_a2, rwkv_g2, rwkv_k_k, rwkv_k_a, rwkv_r_k, rwkv_ln_w, rwkv_ln_b, w_out, ffn_norm, ffn_w_gate, ffn_w_up, ffn_w_down, final_norm):
    bsz, seq, d = x.shape
    depth = w_in.shape[0]
    x2d = x.reshape(bsz * seq, d)
    w_perm = _perm_w_in(w_in)
    for l in range(depth):
        u, (w_out_bf, w_gate_bf, w_up_bf, w_down_bf) = _inproj(
            x2d, attn_norm[l], w_perm, l, (w_out, ffn_w_gate, ffn_w_up, ffn_w_down))
        u = u.reshape(bsz, seq, U_COLS)

        gdn_prm = _rows([_pad_lanes(gdn_a_log[l], 0, LANES), _pad_lanes(gdn_dt_bias[l], 0, LANES),
                         gdn_norm[l]], LANES)
        ml_bias = _pad_lanes(mlstm_b_i[l], 2 * N_HEADS, LANES) + _pad_lanes(mlstm_b_f[l], 3 * N_HEADS, LANES)
        ml_prm = _rows([ml_bias] + [mlstm_norm[l, h] for h in range(N_HEADS)], LANES)
        rg_prm = _rows([rglru_conv_b[l], rglru_b_a[l], rglru_b_x[l], rglru_lambda[l]], GROUP_W)
        rw_prm = _rows([rwkv_w0[l], rwkv_a0[l], rwkv_k_k[l], rwkv_k_a[l], rwkv_r_k[l],
                        rwkv_ln_w[l], rwkv_ln_b[l]], GROUP_W)
        zeros_lora = jnp.zeros((RWKV_N, GROUP_W), F32)
        w_lora = jnp.concatenate([
            jnp.concatenate([rwkv_w2[l], zeros_lora], axis=1),
            jnp.concatenate([zeros_lora, rwkv_a2[l]], axis=1)], axis=0).astype(BF16)
        x2d = _mixers(u, x2d.reshape(bsz, seq, d), w_out_bf, 0,
                      gdn_conv[l], gdn_prm, mlstm_conv[l], ml_prm, rglru_conv[l], rg_prm,
                      rglru_w_a[l].astype(BF16), rglru_w_x[l].astype(BF16),
                      rwkv_mu[l].reshape(1, RWKV_USED), rw_prm, w_lora, rwkv_g2[l].astype(BF16)
                      ).reshape(bsz * seq, d)
        x2d = _ffn(x2d, ffn_norm[l], w_gate_bf, w_up_bf, w_down_bf, final_norm, 0, final=(l == depth - 1))
    return x2d.reshape(bsz, seq, d)
```

```python
import functools
import itertools

import jax
import jax.numpy as jnp
from jax import lax
from jax.experimental import pallas as pl
from jax.experimental.pallas import tpu as pltpu

F32 = jnp.float32
BF16 = jnp.bfloat16

D_MODEL = 2048
GROUP_W = 512
HEAD_D = 128
N_HEADS = 4
RWKV_N = 64
N_PAIRS = GROUP_W // (2 * RWKV_N)
CONV_K = 4
CHUNK = 64
FFN_HIDDEN = 5632
NORM_EPS = 1e-6
RWKV_LN_EPS = 64e-5
RGLRU_C = 8.0
LANES = 128
HALO = 8
INV_LEVELS = 5

U_COLS = 7168
GDN_C0 = 0
MLSTM_C0 = 2048
RWKV_C0 = 4096
RWKV_USED = 1792
GATES_C0 = 5888
RGLRU_C0 = 6144
OUT_GDN, OUT_MLSTM, OUT_RGLRU, OUT_RWKV = 0, GROUP_W, 2 * GROUP_W, 3 * GROUP_W

VMEM_LIMIT = 60 * 1024 * 1024

_NN = (((1,), (0,)), ((), ()))
_NT = (((1,), (1,)), ((), ()))
_TN = (((0,), (0,)), ((), ()))


def _dot(a, b, dims=_NN):
    return lax.dot_general(a, b, dims, preferred_element_type=F32)


def _mm(a, b, dims=_NN):
    return _dot(a.astype(BF16), b.astype(BF16), dims)


def _sigmoid(x):
    return 1.0 / (1.0 + jnp.exp(-x))


def _softplus(x):
    return jnp.maximum(x, 0.0) + jnp.log(1.0 + jnp.exp(-jnp.abs(x)))


def _silu(x):
    return x * _sigmoid(x)


def _rms(x, w, eps=NORM_EPS):
    return x * lax.rsqrt(jnp.mean(x * x, axis=-1, keepdims=True) + eps) * w


def _chunk_cumsum(x):
    row = lax.broadcasted_iota(jnp.int32, x.shape, 0) % CHUNK
    d = 1
    while d < CHUNK:
        x = x + jnp.where(row >= d, pltpu.roll(x, d, 0), 0.0)
        d *= 2
    return x


def _chunk_cumsum_mxu(x):
    ri = lax.broadcasted_iota(jnp.int32, (CHUNK, CHUNK), 0)
    ci = lax.broadcasted_iota(jnp.int32, (CHUNK, CHUNK), 1)
    tri = (ri >= ci).astype(BF16)
    hi = x.astype(BF16)
    r1 = x - hi.astype(F32)
    mid = r1.astype(BF16)
    lo = (r1 - mid.astype(F32)).astype(BF16)
    outs = []
    for c in range(x.shape[0] // CHUNK):
        rows = _brows(c)
        outs.append(_dot(tri, hi[rows, :]) + (_dot(tri, mid[rows, :]) + _dot(tri, lo[rows, :])))
    return jnp.concatenate(outs, axis=0)


def _split(a):
    hi = a.astype(BF16)
    return hi, (a - hi.astype(F32)).astype(BF16)


def _mm3(a, b):
    return _dot(a[0], b[0]) + (_dot(a[0], b[1]) + _dot(a[1], b[0]))


def _inv_unit_lower_steps(a_list, out):
    n = a_list[0].shape[0]
    eye = (lax.broadcasted_iota(jnp.int32, (n, n), 0) == lax.broadcasted_iota(jnp.int32, (n, n), 1)).astype(F32)
    inv = [eye - a for a in a_list]
    p = [(-a).astype(BF16) for a in a_list]
    for _ in range(INV_LEVELS):
        p = [_dot(x, x).astype(BF16) for x in p]
        yield
        inv = [x + _dot(x.astype(BF16), y) for x, y in zip(inv, p)]
        yield
    a_s = [_split(a) for a in a_list]
    t_s = [_split(t) for t in inv]
    a_t = [_mm3(x, y) for x, y in zip(a_s, t_s)]
    yield
    out[:] = [t + _dot(ts[0], (eye - t - x).astype(BF16)) for t, ts, x in zip(inv, t_s, a_t)]


def _causal_conv(buf, x, conv_w):
    tt = x.shape[0]
    buf[HALO:HALO + tt, :] = x
    xx = buf[...]
    acc = conv_w[0:1, :] * xx
    for j in range(1, CONV_K):
        acc = conv_w[j:j + 1, :] * xx + pltpu.roll(acc, 1, 0)
    buf[0:HALO, :] = x[tt - HALO:tt, :]
    return acc[HALO:, :]


def _brows(b):
    return slice(b * CHUNK, (b + 1) * CHUNK)


def _hcols(h, base=0):
    return slice(base + h * HEAD_D, base + (h + 1) * HEAD_D)


def _inproj_kernel(x_ref, nw_ref, w_ref, *refs, n_cast):
    cast_in, o_ref, cast_out, n_scr = refs[:n_cast], refs[n_cast], refs[n_cast + 1:2 * n_cast + 1], refs[-1]

    @pl.when(pl.program_id(1) == 0)
    def _():
        n_scr[...] = _rms(x_ref[...], nw_ref[...]).astype(BF16)

    o_ref[...] = _dot(n_scr[...], w_ref[...], _NT)
    for src, dst in zip(cast_in, cast_out):
        dst[...] = src[...].astype(BF16)


def _cast_block_rows(rows, n_steps):
    rb = 16
    while rows % rb or rows // rb > n_steps:
        rb += 16
    return rb


def _inproj(x2d, norm_w, w_perm, layer, cast_ws, *, tm=1024, tn=1792):
    n_tok = x2d.shape[0]
    n_i, n_j = n_tok // tm, U_COLS // tn
    in_specs = [
        pl.BlockSpec((tm, D_MODEL), lambda i, j: (i, 0)),
        pl.BlockSpec((1, D_MODEL), lambda i, j: (0, 0)),
        pl.BlockSpec((None, tn, D_MODEL), lambda i, j: (layer, j, 0)),
    ]
    out_specs = [pl.BlockSpec((tm, tn), lambda i, j: (i, j))]
    out_shape = [jax.ShapeDtypeStruct((n_tok, U_COLS), F32)]
    for w in cast_ws:
        _, rows, cols = w.shape
        rb = _cast_block_rows(rows, n_i * n_j)

        def blk(i, j, _last=rows // rb - 1):
            return jnp.minimum(i * n_j + j, _last)

        in_specs.append(pl.BlockSpec((None, rb, cols), lambda i, j, _b=blk: (layer, _b(i, j), 0)))
        out_specs.append(pl.BlockSpec((None, rb, cols), lambda i, j, _b=blk: (0, _b(i, j), 0)))
        out_shape.append(jax.ShapeDtypeStruct((1, rows, cols), BF16))
    outs = pl.pallas_call(
        functools.partial(_inproj_kernel, n_cast=len(cast_ws)),
        out_shape=out_shape,
        grid=(n_i, n_j),
        in_specs=in_specs,
        out_specs=out_specs,
        scratch_shapes=[pltpu.VMEM((tm, D_MODEL), BF16)],
        compiler_params=pltpu.CompilerParams(
            dimension_semantics=("arbitrary", "arbitrary"), vmem_limit_bytes=VMEM_LIMIT),
        name="inproj",
    )(x2d, norm_w.reshape(1, D_MODEL), w_perm, *cast_ws)
    return outs[0], outs[1:]


def _gdn_steps(u_ref, cw_ref, prm_ref, o_ref, buf, s_ref):
    nb = u_ref.shape[0]
    cw = cw_ref[...]
    qkv = _silu(jnp.concatenate(
        [_causal_conv(buf.at[b], u_ref[b, :, GDN_C0:GDN_C0 + 3 * GROUP_W], cw) for b in range(nb)], axis=0))
    yield
    gates = u_ref[:, :, GATES_C0:GATES_C0 + LANES].reshape(nb * CHUNK, LANES)
    prm = prm_ref[...]
    g_all = -jnp.exp(prm[0:1, :]) * _softplus(gates + prm[1:2, :])
    beta_all = _sigmoid(gates)
    gc_all = _chunk_cumsum(g_all)
    norm_w = prm[2:3, :]
    gc_t = [gc_all[_brows(b), :].T for b in range(nb)]
    yield

    qn, kn = [], []
    for h in range(N_HEADS):
        q = qkv[:, _hcols(h)]
        k = qkv[:, _hcols(h, GROUP_W)]
        qn.append(q * (lax.rsqrt(jnp.sum(q * q, axis=-1, keepdims=True) + 1e-6) * (HEAD_D ** -0.5)))
        kn.append(k * lax.rsqrt(jnp.sum(k * k, axis=-1, keepdims=True) + 1e-6))
    yield

    ri = lax.broadcasted_iota(jnp.int32, (CHUNK, CHUNK), 0)
    ci = lax.broadcasted_iota(jnp.int32, (CHUNK, CHUNK), 1)
    causal = ri >= ci
    strict = ri > ci

    probs = [(b, h) for b in range(nb) for h in range(N_HEADS)]
    q = [qn[h][_brows(b), :] for b, h in probs]
    k = [kn[h][_brows(b), :] for b, h in probs]
    v = [qkv[_brows(b), _hcols(h, 2 * GROUP_W)] for b, h in probs]
    beta = [beta_all[_brows(b), N_HEADS + h:N_HEADS + h + 1] for b, h in probs]
    gcol = [gc_all[_brows(b), h:h + 1] for b, h in probs]
    grow = [gc_t[b][h:h + 1, :] for b, h in probs]
    decay = [jnp.where(causal, jnp.exp(jnp.where(causal, gc - gr, 0.0)), 0.0) for gc, gr in zip(gcol, grow)]
    yield
    kb = [x * y for x, y in zip(k, beta)]
    raw = [_mm(jnp.concatenate([x, y], axis=0), z, _NT) for x, y, z in zip(kb, q, k)]
    yield
    a_mat = [jnp.where(strict, r[:CHUNK] * d, 0.0) for r, d in zip(raw, decay)]
    attn = [r[CHUNK:] * d for r, d in zip(raw, decay)]
    t_mat = []
    yield from _inv_unit_lower_steps(a_mat, t_mat)
    eg = [jnp.exp(x) for x in gcol]
    sol = [_mm(t, jnp.concatenate([x * y, z * e], axis=1))
           for t, x, y, z, e in zip(t_mat, v, beta, kb, eg)]
    yield
    s_old = [s_ref[b, h] for b, h in probs]
    ws = [_mm(jnp.concatenate([x[:, HEAD_D:], y * e], axis=0), s)
          for x, y, e, s in zip(sol, q, eg, s_old)]
    yield
    v_new = [x[:, :HEAD_D] - y[:CHUNK] for x, y in zip(sol, ws)]
    o = [y[CHUNK:] + _mm(a, x) for y, a, x in zip(ws, attn, v_new)]
    yield
    glast = [x[CHUNK - 1:CHUNK, :] for x in gcol]
    s_new = [s * jnp.exp(gl) + _mm(x * jnp.exp(gl - gc), vn, _TN)
             for s, gl, x, gc, vn in zip(s_old, glast, k, gcol, v_new)]
    for (b, h), s in zip(probs, s_new):
        s_ref[b, h] = s
    yield
    for (b, h), x in zip(probs, o):
        z = u_ref[b, :, _hcols(h, GDN_C0 + 3 * GROUP_W)]
        o_ref[b, :, _hcols(h, OUT_GDN)] = (_rms(x, norm_w) * _silu(z)).astype(BF16)


def _mlstm_steps(u_ref, cw_ref, prm_ref, o_ref, buf, c_ref, n_ref, m_ref):
    nb = u_ref.shape[0]
    cw = cw_ref[...]
    qk = _silu(jnp.concatenate(
        [_causal_conv(buf.at[b], u_ref[b, :, MLSTM_C0:MLSTM_C0 + 2 * GROUP_W], cw) for b in range(nb)], axis=0))
    yield
    gates = u_ref[:, :, GATES_C0:GATES_C0 + LANES].reshape(nb * CHUNK, LANES)
    prm = prm_ref[...]
    pre = gates + prm[0:1, :]
    b_all = _chunk_cumsum(-_softplus(-pre))
    b_tr = [b_all[_brows(b), :].T for b in range(nb)]
    i_tr = [pre[_brows(b), :].T for b in range(nb)]
    yield

    ri = lax.broadcasted_iota(jnp.int32, (CHUNK, CHUNK), 0)
    ci = lax.broadcasted_iota(jnp.int32, (CHUNK, CHUNK), 1)
    causal = ri >= ci

    probs = [(b, h) for b in range(nb) for h in range(N_HEADS)]
    li = [2 * N_HEADS + h for _, h in probs]
    lf = [3 * N_HEADS + h for _, h in probs]
    q = [qk[_brows(b), _hcols(h)] for b, h in probs]
    k = [qk[_brows(b), _hcols(h, GROUP_W)] * (HEAD_D ** -0.5) for b, h in probs]
    v = [u_ref[b, :, _hcols(h, MLSTM_C0 + 2 * GROUP_W)] for b, h in probs]
    bcol = [b_all[_brows(b), l:l + 1] for (b, _), l in zip(probs, lf)]
    icol = [pre[_brows(b), l:l + 1] for (b, _), l in zip(probs, li)]
    brow = [b_tr[b][l:l + 1, :] for (b, _), l in zip(probs, lf)]
    irow = [i_tr[b][l:l + 1, :] for (b, _), l in zip(probs, li)]
    m_old = [m_ref[b, h][0:1, 0:1] for b, h in probs]
    c_old = [c_ref[b, h] for b, h in probs]
    n_old = [n_ref[b, h][0:1, :] for b, h in probs]
    yield

    d = [jnp.where(causal, bc - br + ir, -jnp.inf) for bc, br, ir in zip(bcol, brow, irow)]
    inter = [bc + m for bc, m in zip(bcol, m_old)]
    m_t = [jnp.maximum(x, jnp.max(y, axis=-1, keepdims=True)) for x, y in zip(inter, d)]
    yield
    qk_raw = [_mm(x, y, _NT) for x, y in zip(q, k)]
    qc = [_mm(x, c) for x, c in zip(q, c_old)]
    yield
    s = [r * jnp.exp(x - m) for r, x, m in zip(qk_raw, d, m_t)]
    a = [jnp.exp(x - m) for x, m in zip(inter, m_t)]
    yield
    sv = [_mm(x, y) for x, y in zip(s, v)]
    yield
    num = [x * y + z for x, y, z in zip(a, qc, sv)]
    den = [x * jnp.sum(y * n, axis=-1, keepdims=True) + jnp.sum(z, axis=-1, keepdims=True)
           for x, y, n, z in zip(a, q, n_old, s)]
    h_t = [x / jnp.maximum(jnp.abs(y), jnp.exp(-m)) for x, y, m in zip(num, den, m_t)]
    yield
    g = [x[CHUNK - 1:CHUNK, :] for x in bcol]
    w_log = [x - bc + ic for x, bc, ic in zip(g, bcol, icol)]
    m_new = [jnp.maximum(x + m, jnp.max(w, axis=0, keepdims=True)) for x, m, w in zip(g, m_old, w_log)]
    scale = [jnp.exp(x + m - mn) for x, m, mn in zip(g, m_old, m_new)]
    kw = [x * jnp.exp(w - mn) for x, w, mn in zip(k, w_log, m_new)]
    yield
    c_new = [sc * c + _mm(x, y, _TN) for sc, c, x, y in zip(scale, c_old, kw, v)]
    n_new = [sc * n + jnp.sum(x, axis=0, keepdims=True) for sc, n, x in zip(scale, n_old, kw)]
    for (b, h), c, n, m in zip(probs, c_new, n_new, m_new):
        c_ref[b, h] = c
        n_ref[b, h] = jnp.broadcast_to(n, (8, HEAD_D))
        m_ref[b, h] = jnp.broadcast_to(m, (8, HEAD_D))
    yield
    for (b, h), x in zip(probs, h_t):
        o_gate = u_ref[b, :, _hcols(h, MLSTM_C0 + 3 * GROUP_W)]
        o_ref[b, :, _hcols(h, OUT_MLSTM)] = _rms(_sigmoid(o_gate) * x, prm[1 + h:2 + h, :]).astype(BF16)


def _rglru_steps(u_ref, cw_ref, prm_ref, wa_ref, wx_ref, o_ref, buf, h_ref):
    nb = u_ref.shape[0]
    prm = prm_ref[...]
    cw = cw_ref[...]
    xb = jnp.concatenate(
        [_causal_conv(buf.at[b], u_ref[b, :, RGLRU_C0:RGLRU_C0 + GROUP_W], cw) for b in range(nb)], axis=0)
    xb = xb + prm[0:1, :]
    yield
    ra = []
    rx = []
    for n in range(N_HEADS):
        blk = xb[:, _hcols(n)].astype(BF16)
        ra.append(_dot(blk, wa_ref[n]))
        rx.append(_dot(blk, wx_ref[n]))
    yield
    r = _sigmoid(jnp.concatenate(ra, axis=1) + prm[1:2, :])
    i = _sigmoid(jnp.concatenate(rx, axis=1) + prm[2:3, :])
    log_a = -RGLRU_C * r * _softplus(-prm[3:4, :])
    a = jnp.exp(log_a)
    th = jnp.tanh(log_a)
    bb = jnp.sqrt(-2.0 * th / (1.0 - th)) * (i * xb)
    yield
    row = lax.broadcasted_iota(jnp.int32, a.shape, 0) % HALO
    d = 1
    while d < HALO:
        keep = row >= d
        bb = jnp.where(keep, a * pltpu.roll(bb, d, 0) + bb, bb)
        a = jnp.where(keep, a * pltpu.roll(a, d, 0), a)
        d *= 2
        yield
    carry = [h_ref[b][0:1, :] for b in range(nb)]
    groups = [[] for _ in range(nb)]
    for g in range(CHUNK // HALO):
        for b in range(nb):
            lo = b * CHUNK + g * HALO
            hg = bb[lo:lo + HALO, :] + a[lo:lo + HALO, :] * carry[b]
            groups[b].append(hg)
            carry[b] = hg[HALO - 1:HALO, :]
    yield
    for b in range(nb):
        h_ref[b] = jnp.broadcast_to(carry[b], (8, GROUP_W))
        gate = u_ref[b, :, RGLRU_C0 + GROUP_W:RGLRU_C0 + 2 * GROUP_W]
        gelu = 0.5 * gate * (1.0 + jnp.tanh(0.7978845608028654 * (gate + 0.044715 * gate * gate * gate)))
        o_ref[b, :, OUT_RGLRU:OUT_RGLRU + GROUP_W] = (jnp.concatenate(groups[b], axis=0) * gelu).astype(BF16)


def _pair_sum(x, lo_mask):
    s_lo = jnp.sum(jnp.where(lo_mask, x, 0.0), axis=-1, keepdims=True)
    s_hi = jnp.sum(jnp.where(lo_mask, 0.0, x), axis=-1, keepdims=True)
    return jnp.where(lo_mask, s_lo, s_hi)


def _stack_heads(x, lo_mask):
    return jnp.concatenate([jnp.where(lo_mask, x, 0.0), jnp.where(lo_mask, 0.0, x)], axis=0)


def _rwkv_steps(u_ref, mu_ref, prm_ref, wlo_ref, g2_ref, o_ref, buf, s_ref):
    nb = u_ref.shape[0]
    mu = mu_ref[...]
    xs = []
    for b in range(nb):
        x = u_ref[b, :, RWKV_C0:RWKV_C0 + RWKV_USED]
        buf[b, HALO:HALO + CHUNK, :] = x
        x_prev = buf[b, HALO - 1:HALO - 1 + CHUNK, :]
        buf[b, 0:HALO, :] = x[CHUNK - HALO:CHUNK, :]
        xs.append(x + (x_prev - x) * mu)
    xs = jnp.concatenate(xs, axis=0)
    yield
    prm = prm_ref[...]
    r_all = xs[:, 0:GROUP_W]
    k_all = xs[:, GROUP_W:2 * GROUP_W]
    v_all = xs[:, 2 * GROUP_W:3 * GROUP_W]
    lora = xs[:, 3 * GROUP_W:3 * GROUP_W + LANES]
    lane = lax.broadcasted_iota(jnp.int32, lora.shape, 1)
    lora = jnp.where(lane < RWKV_N, jnp.tanh(lora), lora)
    wa = _mm(lora, wlo_ref[...])
    g_all = _mm(_sigmoid(xs[:, 3 * GROUP_W + LANES:3 * GROUP_W + 2 * LANES]), g2_ref[...])
    yield
    w_raw = -_softplus(-(prm[0:1, :] + wa[:, :GROUP_W])) - 0.5
    lw_all = -jnp.exp(w_raw)
    a_all = _sigmoid(prm[1:2, :] + wa[:, GROUP_W:])
    kk_all = k_all * prm[2:3, :]
    k2_all = k_all * (1.0 + (a_all - 1.0) * prm[3:4, :])
    yield
    lc_all = _chunk_cumsum_mxu(lw_all)
    yield

    lo_mask = lax.broadcasted_iota(jnp.int32, (CHUNK, LANES), 1) < RWKV_N
    n2 = 2 * CHUNK
    ri = lax.broadcasted_iota(jnp.int32, (n2, n2), 0)
    ci = lax.broadcasted_iota(jnp.int32, (n2, n2), 1)
    same = (ri // CHUNK) == (ci // CHUNK)
    strict = same & ((ci % CHUNK) < (ri % CHUNK))
    incl = same & ((ci % CHUNK) <= (ri % CHUNK))

    probs = [(b, p) for b in range(nb) for p in range(N_PAIRS)]

    def sel(arr):
        return [arr[_brows(b), p * LANES:(p + 1) * LANES] for b, p in probs]

    r, v, k2, a_sig, lw, lc = sel(r_all), sel(v_all), sel(k2_all), sel(a_all), sel(lw_all), sel(lc_all)
    kk = [x * lax.rsqrt(_pair_sum(x * x, lo_mask) + 1e-6) for x in sel(kk_all)]
    yield
    w_incl = [jnp.exp(x) for x in lc]
    w_inv = [jnp.exp(-x) for x in lc]
    a_hat = [_stack_heads(-x * jnp.exp(c - w), lo_mask) for x, c, w in zip(kk, lc, lw)]
    b_hat = [_stack_heads(x * y * w, lo_mask) for x, y, w in zip(kk, a_sig, w_inv)]
    yield
    k_hat = [_stack_heads(x * w, lo_mask) for x, w in zip(k2, w_inv)]
    r_hat = [_stack_heads(x * w, lo_mask) for x, w in zip(r, w_incl)]
    v_st = [_stack_heads(x, lo_mask) for x in v]
    s_old = [s_ref[b, p] for b, p in probs]
    yield

    ar = [jnp.concatenate([x, y], axis=0).astype(BF16) for x, y in zip(a_hat, r_hat)]
    bk = [jnp.concatenate([x, y], axis=0).astype(BF16) for x, y in zip(b_hat, k_hat)]
    cross = [_dot(x, y, _NT) for x, y in zip(ar, bk)]
    yield
    a_ab = [jnp.where(strict, x[:n2, :n2], 0.0) for x in cross]
    a_ak = [jnp.where(strict, x[:n2, n2:], 0.0) for x in cross]
    m_r = [jnp.concatenate([jnp.where(incl, x[n2:, :n2], 0.0), jnp.where(incl, x[n2:, n2:], 0.0)], axis=1)
           for x in cross]
    yield
    ars = [_dot(x, s.astype(BF16), _NT) for x, s in zip(ar, s_old)]
    akv = [_mm(y, z) for y, z in zip(a_ak, v_st)]
    t_mat = []
    yield from _inv_unit_lower_steps([-x for x in a_ab], t_mat)
    pv = [_mm(t, x[:n2] + y) for t, x, y in zip(t_mat, ars, akv)]
    yield
    pvv = [jnp.concatenate([x, y], axis=0).astype(BF16) for x, y in zip(pv, v_st)]
    y_st = [x[n2:] + _dot(m.astype(BF16), z) for x, m, z in zip(ars, m_r, pvv)]
    yield
    s_new = [(s + _dot(x, y, _TN)) * w[CHUNK - 1:CHUNK, :] for s, x, y, w in zip(s_old, pvv, bk, w_incl)]
    for (b, p), s in zip(probs, s_new):
        s_ref[b, p] = s
    yield
    for (b, p), ys, rr, kk2, vv in zip(probs, y_st, r, k2, v):
        cols = slice(p * LANES, (p + 1) * LANES)
        y = ys[:CHUNK, :] + ys[CHUNK:, :]
        yc = y - _pair_sum(y, lo_mask) * (1.0 / RWKV_N)
        var = _pair_sum(yc * yc, lo_mask) * (1.0 / RWKV_N)
        y = yc * lax.rsqrt(var + RWKV_LN_EPS) * prm[5:6, cols] + prm[6:7, cols]
        y = y + _pair_sum(rr * kk2 * prm[4:5, cols], lo_mask) * vv
        o_ref[b, :, OUT_RWKV + p * LANES:OUT_RWKV + (p + 1) * LANES] = (y * g_all[_brows(b), cols]).astype(BF16)


OUTPROJ_TILES = 4


def _outproj_steps(x_ref, mix_ref, w_ref, o_ref):
    nb = x_ref.shape[0]
    mix = mix_ref[...].reshape(nb * CHUNK, D_MODEL)
    tn = D_MODEL // OUTPROJ_TILES
    for j in range(OUTPROJ_TILES):
        cols = slice(j * tn, (j + 1) * tn)
        o_ref[:, :, cols] = x_ref[:, :, cols] + _dot(mix, w_ref[:, cols]).reshape(nb, CHUNK, tn)
        yield


def _mixers_kernel(u_ref, x_ref, wout_ref, gdn_cw, gdn_prm, ml_cw, ml_prm, rg_cw, rg_prm, rg_wa, rg_wx,
                   rw_mu, rw_prm, rw_wlo, rw_g2, o_ref,
                   mix, gdn_buf, gdn_s, ml_buf, ml_c, ml_n, ml_m, rg_buf, rg_h, rw_buf, rw_s):
    @pl.when(pl.program_id(0) == 0)
    def _():
        for buf in (gdn_buf, ml_buf, rg_buf, rw_buf):
            buf[:, 0:HALO, :] = jnp.zeros((buf.shape[0], HALO, buf.shape[2]), F32)
        for state in (mix, gdn_s, ml_c, ml_n, ml_m, rg_h, rw_s):
            state[...] = jnp.zeros_like(state)

    last = pl.num_programs(0) - 1

    @pl.when(pl.program_id(0) < last)
    def _():
        stages = [
            _outproj_steps(x_ref, mix, wout_ref, o_ref),
            _rwkv_steps(u_ref, rw_mu, rw_prm, rw_wlo, rw_g2, mix, rw_buf, rw_s),
            itertools.chain(_mlstm_steps(u_ref, ml_cw, ml_prm, mix, ml_buf, ml_c, ml_n, ml_m),
                            _gdn_steps(u_ref, gdn_cw, gdn_prm, mix, gdn_buf, gdn_s),
                            _rglru_steps(u_ref, rg_cw, rg_prm, rg_wa, rg_wx, mix, rg_buf, rg_h)),
        ]
        n_rounds = [0] * len(stages)
        for done in itertools.zip_longest(*stages, fillvalue="done"):
            for i, d in enumerate(done):
                n_rounds[i] += d != "done"
        assert all(n > n_rounds[0] for n in n_rounds[1:]), n_rounds

    @pl.when(pl.program_id(0) == last)
    def _():
        for _ in _outproj_steps(x_ref, mix, wout_ref, o_ref):
            pass


def _mixers(u, x, w_out, layer, gdn_cw, gdn_prm, ml_cw, ml_prm, rg_cw, rg_prm, rg_wa, rg_wx,
            rw_mu, rw_prm, rw_wlo, rw_g2):
    bsz, seq, _ = u.shape
    n_chunks = seq // CHUNK

    def whole(a):
        return pl.BlockSpec(a.shape, lambda t, _n=a.ndim: (0,) * _n)

    def prev_chunk(t):
        return (0, jnp.maximum(t - 1, 0), 0)

    params = (gdn_cw, gdn_prm, ml_cw, ml_prm, rg_cw, rg_prm, rg_wa, rg_wx, rw_mu, rw_prm, rw_wlo, rw_g2)
    return pl.pallas_call(
        _mixers_kernel,
        out_shape=jax.ShapeDtypeStruct((bsz, seq, D_MODEL), F32),
        grid=(n_chunks + 1,),
        in_specs=[pl.BlockSpec((bsz, CHUNK, U_COLS), lambda t: (0, jnp.minimum(t, n_chunks - 1), 0)),
                  pl.BlockSpec((bsz, CHUNK, D_MODEL), prev_chunk),
                  pl.BlockSpec((None, D_MODEL, D_MODEL), lambda t: (layer, 0, 0))] + [whole(a) for a in params],
        out_specs=pl.BlockSpec((bsz, CHUNK, D_MODEL), prev_chunk),
        scratch_shapes=[
            pltpu.VMEM((bsz, CHUNK, D_MODEL), BF16),
            pltpu.VMEM((bsz, HALO + CHUNK, 3 * GROUP_W), F32),
            pltpu.VMEM((bsz, N_HEADS, HEAD_D, HEAD_D), F32),
            pltpu.VMEM((bsz, HALO + CHUNK, 2 * GROUP_W), F32),
            pltpu.VMEM((bsz, N_HEADS, HEAD_D, HEAD_D), F32),
            pltpu.VMEM((bsz, N_HEADS, 8, HEAD_D), F32),
            pltpu.VMEM((bsz, N_HEADS, 8, HEAD_D), F32),
            pltpu.VMEM((bsz, HALO + CHUNK, GROUP_W), F32),
            pltpu.VMEM((bsz, 8, GROUP_W), F32),
            pltpu.VMEM((bsz, HALO + CHUNK, RWKV_USED), F32),
            pltpu.VMEM((bsz, N_PAIRS, LANES, LANES), F32),
        ],
        compiler_params=pltpu.CompilerParams(
            dimension_semantics=("arbitrary",), vmem_limit_bytes=VMEM_LIMIT),
        name="mixers",
    )(u, x, w_out, *params)


def _ffn_kernel(x_ref, nw_ref, wg_ref, wu_ref, wd_ref, fw_ref, o_ref, n_scr, *, n_f, final):
    f = pl.program_id(1)

    @pl.when(f == 0)
    def _():
        x = x_ref[...]
        n_scr[...] = _rms(x, nw_ref[...]).astype(BF16)
        o_ref[...] = x

    n = n_scr[...]
    gate = jnp.dot(n, wg_ref[...], preferred_element_type=F32)
    up = jnp.dot(n, wu_ref[...], preferred_element_type=F32)
    hid = (_silu(gate) * up).astype(BF16)
    o_ref[...] += jnp.dot(hid, wd_ref[...], preferred_element_type=F32)

    if final:
        @pl.when(f == n_f - 1)
        def _():
            o_ref[...] = _rms(o_ref[...], fw_ref[...])


def _ffn(x2d, norm_w, w_gate, w_up, w_down, final_w, layer, *, final, tm=1024, tf=512):
    n_tok = x2d.shape[0]
    n_f = FFN_HIDDEN // tf
    return pl.pallas_call(
        functools.partial(_ffn_kernel, n_f=n_f, final=final),
        out_shape=jax.ShapeDtypeStruct((n_tok, D_MODEL), F32),
        grid=(n_tok // tm, n_f),
        in_specs=[
            pl.BlockSpec((tm, D_MODEL), lambda i, f: (i, 0)),
            pl.BlockSpec((1, D_MODEL), lambda i, f: (0, 0)),
            pl.BlockSpec((None, D_MODEL, tf), lambda i, f: (layer, 0, f)),
            pl.BlockSpec((None, D_MODEL, tf), lambda i, f: (layer, 0, f)),
            pl.BlockSpec((None, tf, D_MODEL), lambda i, f: (layer, f, 0)),
            pl.BlockSpec((1, D_MODEL), lambda i, f: (0, 0)),
        ],
        out_specs=pl.BlockSpec((tm, D_MODEL), lambda i, f: (i, 0)),
        scratch_shapes=[pltpu.VMEM((tm, D_MODEL), BF16)],
        compiler_params=pltpu.CompilerParams(
            dimension_semantics=("parallel", "arbitrary"), vmem_limit_bytes=VMEM_LIMIT),
        name="ffn_final" if final else "ffn",
    )(x2d, norm_w.reshape(1, D_MODEL), w_gate, w_up, w_down, final_w.reshape(1, D_MODEL))


def _pad_lanes(v, offset, width):
    v = v.astype(F32)
    return jnp.concatenate([jnp.zeros((offset,), F32), v, jnp.zeros((width - offset - v.shape[0],), F32)])


def _rows(rows, width):
    rows = [r.astype(F32).reshape(width) for r in rows]
    return jnp.stack(rows + [jnp.zeros((width,), F32)] * (8 - len(rows)))


def _perm_w_in(w):
    wt = jnp.swapaxes(w, 1, 2).astype(BF16)
    pad = jnp.zeros((w.shape[0], RGLRU_C0 - GATES_C0 - 16, w.shape[1]), BF16)
    return jnp.concatenate([wt[:, 0:2048], wt[:, 2056:4104], wt[:, 5136:6928], wt[:, 2048:2056], wt[:, 4104:4112],
                            pad, wt[:, 4112:5136]], axis=1)


def kernel(x, attn_norm, w_in, gdn_conv, gdn_a_log, gdn_dt_bias, gdn_norm, mlstm_conv, mlstm_b_i, mlstm_b_f, mlstm_norm, rglru_conv, rglru_conv_b, rglru_w_a, rglru_b_a, rglru_w_x, rglru_b_x, rglru_lambda, rwkv_mu, rwkv_w0, rwkv_w2, rwkv_a0, rwkv_a2, rwkv_g2, rwkv_k_k, rwkv_k_a, rwkv_r_k, rwkv_ln_w, rwkv_ln_b, w_out, ffn_norm, ffn_w_gate, ffn_w_up, ffn_w_down, final_norm):
    bsz, seq, d = x.shape
    depth = w_in.shape[0]
    x2d = x.reshape(bsz * seq, d)
    w_perm = _perm_w_in(w_in)
    for l in range(depth):
        u, (w_out_bf, w_gate_bf, w_up_bf, w_down_bf) = _inproj(
            x2d, attn_norm[l], w_perm, l, (w_out, ffn_w_gate, ffn_w_up, ffn_w_down))
        u = u.reshape(bsz, seq, U_COLS)

        gdn_prm = _rows([_pad_lanes(gdn_a_log[l], 0, LANES), _pad_lanes(gdn_dt_bias[l], 0, LANES),
                         gdn_norm[l]], LANES)
        ml_bias = _pad_lanes(mlstm_b_i[l], 2 * N_HEADS, LANES) + _pad_lanes(mlstm_b_f[l], 3 * N_HEADS, LANES)
        ml_prm = _rows([ml_bias] + [mlstm_norm[l, h] for h in range(N_HEADS)], LANES)
        rg_prm = _rows([rglru_conv_b[l], rglru_b_a[l], rglru_b_x[l], rglru_lambda[l]], GROUP_W)
        rw_prm = _rows([rwkv_w0[l], rwkv_a0[l], rwkv_k_k[l], rwkv_k_a[l], rwkv_r_k[l],
                        rwkv_ln_w[l], rwkv_ln_b[l]], GROUP_W)
        zeros_lora = jnp.zeros((RWKV_N, GROUP_W), F32)
        w_lora = jnp.concatenate([
            jnp.concatenate([rwkv_w2[l], zeros_lora], axis=1),
            jnp.concatenate([zeros_lora, rwkv_a2[l]], axis=1)], axis=0).astype(BF16)
        x2d = _mixers(u, x2d.reshape(bsz, seq, d), w_out_bf, 0,
                      gdn_conv[l], gdn_prm, mlstm_conv[l], ml_prm, rglru_conv[l], rg_prm,
                      rglru_w_a[l].astype(BF16), rglru_w_x[l].astype(BF16),
                      rwkv_mu[l].reshape(1, RWKV_USED), rw_prm, w_lora, rwkv_g2[l].astype(BF16)
                      ).reshape(bsz * seq, d)
        x2d = _ffn(x2d, ffn_norm[l], w_gate_bf, w_up_bf, w_down_bf, final_norm, 0, final=(l == depth - 1))
    return x2d.reshape(bsz, seq, d)
```
